```python
import math
import jax, jax.numpy as jnp
from jax import lax
import numpy as np

D_MODEL = 1024
BATCH = 8
SEQ = 2048
DEPTH = 4

N_MIXERS = 2
N_A = (DEPTH + 1) // 2
N_B = DEPTH // 2
CONV_WIDTH = 31
POOL_WINDOWS = (2, 4, 8, 16)
N_POOL_GROUPS = len(POOL_WINDOWS)
POOL_GROUP = D_MODEL // N_POOL_GROUPS
D_FF = int(math.ceil(8 * D_MODEL / 3 / 256) * 256)
DN_ALPHA = float((2 * DEPTH) ** 0.25)
DN_BETA = float((8 * DEPTH) ** -0.25)
LN_EPS = 1e-5

kernel_name = "hybrid_conformer_conv_multiscale_pool_deepnorm"


def layer_norm(x, g, b):
    xf = x.astype(jnp.float32)
    mu = jnp.mean(xf, axis=-1, keepdims=True)
    xc = xf - mu
    var = jnp.mean(xc * xc, axis=-1, keepdims=True)
    y = xc * lax.rsqrt(var + LN_EPS) * g.astype(jnp.float32) + b.astype(jnp.float32)
    return y.astype(x.dtype)


def conformer_conv(x, w_in, b_in, w_dw, b_dw, ln_g, ln_b, w_out, b_out):
    h = jnp.einsum('bsd,de->bse', x, w_in) + b_in
    val, gate = jnp.split(h, 2, axis=-1)
    h = val * jax.nn.sigmoid(gate)
    h = lax.conv_general_dilated(
        h, w_dw, window_strides=(1,), padding=[(CONV_WIDTH - 1, 0)],
        dimension_numbers=('NWC', 'WIO', 'NWC'),
        feature_group_count=D_MODEL) + b_dw
    h = layer_norm(h, ln_g, ln_b)
    h = jax.nn.silu(h)
    return jnp.einsum('bsd,de->bse', h, w_out) + b_out


def multiscale_pool(x, w_grp, b_grp, scale):
    s = x.shape[1]
    pos = jnp.arange(1, s + 1, dtype=jnp.float32)[None, :, None]
    outs = []
    for g, w in enumerate(POOL_WINDOWS):
        xg = x[..., g * POOL_GROUP:(g + 1) * POOL_GROUP].astype(jnp.float32)
        c = jnp.cumsum(xg, axis=1)
        c_lag = jnp.pad(c, ((0, 0), (w, 0), (0, 0)))[:, :s]
        mean = (c - c_lag) / jnp.minimum(pos, float(w))
        outs.append((mean - xg).astype(x.dtype))
    pooled = jnp.stack(outs, axis=2)
    y = jnp.einsum('bsgc,gce->bsge', pooled, w_grp) + b_grp
    y = y.reshape(x.shape)
    return y * scale


def swiglu(x, w_gate, w_up, w_down):
    h = jax.nn.silu(jnp.einsum('bsd,df->bsf', x, w_gate)) * jnp.einsum('bsd,df->bsf', x, w_up)
    return jnp.einsum('bsf,fd->bsd', h, w_down)


def setup_inputs(seed: int = 0) -> dict:
    key = jax.random.key(seed)
    ks = jax.random.split(key, 24)
    f32 = jnp.float32
    D, C = D_MODEL, POOL_GROUP
    nrm = lambda k, shape, s: (jax.random.normal(k, shape, f32) * s).astype(f32)
    return {
        "x": nrm(ks[0], (BATCH, SEQ, D), 1.0),
        "a_w_in": nrm(ks[1], (N_A, D, 2 * D), D ** -0.5),
        "a_b_in": nrm(ks[2], (N_A, 2 * D), 0.01),
        "a_w_dw": nrm(ks[3], (N_A, CONV_WIDTH, 1, D), CONV_WIDTH ** -0.5),
        "a_b_dw": nrm(ks[4], (N_A, D), 0.01),
        "a_ln_g": 1.0 + nrm(ks[5], (N_A, D), 0.01),
        "a_ln_b": nrm(ks[6], (N_A, D), 0.01),
        "a_w_out": nrm(ks[7], (N_A, D, D), DN_BETA * D ** -0.5),
        "a_b_out": nrm(ks[8], (N_A, D), 0.01),
        "p_w": nrm(ks[9], (N_B, N_POOL_GROUPS, C, C), DN_BETA * C ** -0.5),
        "p_b": nrm(ks[10], (N_B, N_POOL_GROUPS, C), 0.01),
        "p_scale": 1.0 + nrm(ks[11], (N_B, D), 0.01),
        "ffn_w_gate": nrm(ks[12], (DEPTH, D, D_FF), D ** -0.5),
        "ffn_w_up": nrm(ks[13], (DEPTH, D, D_FF), D ** -0.5),
        "ffn_w_down": nrm(ks[14], (DEPTH, D_FF, D), DN_BETA * D_FF ** -0.5),
        "ln1_g": 1.0 + nrm(ks[15], (DEPTH, D), 0.01),
        "ln1_b": nrm(ks[16], (DEPTH, D), 0.01),
        "ln2_g": 1.0 + nrm(ks[17], (DEPTH, D), 0.01),
        "ln2_b": nrm(ks[18], (DEPTH, D), 0.01),
    }


def reference(x, a_w_in, a_b_in, a_w_dw, a_b_dw, a_ln_g, a_ln_b, a_w_out, a_b_out,
              p_w, p_b, p_scale, ffn_w_gate, ffn_w_up, ffn_w_down,
              ln1_g, ln1_b, ln2_g, ln2_b):
    alpha = jnp.asarray(DN_ALPHA, x.dtype)
    for i in range(DEPTH):
        j = i // N_MIXERS
        if i % N_MIXERS == 0:
            mix = conformer_conv(x, a_w_in[j], a_b_in[j], a_w_dw[j], a_b_dw[j],
                                 a_ln_g[j], a_ln_b[j], a_w_out[j], a_b_out[j])
        else:
            mix = multiscale_pool(x, p_w[j], p_b[j], p_scale[j])
        x = layer_norm(alpha * x + mix, ln1_g[i], ln1_b[i])
        x = layer_norm(alpha * x + swiglu(x, ffn_w_gate[i], ffn_w_up[i], ffn_w_down[i]),
                       ln2_g[i], ln2_b[i])
    return x
```

```python
import functools
import math

import jax
import jax.numpy as jnp
from jax import lax
from jax.experimental import pallas as pl
from jax.experimental.pallas import tpu as pltpu

F32 = jnp.float32
BF16 = jnp.bfloat16

DEPTH = 4
CONV_WIDTH = 31
POOL_WINDOWS = (2, 4, 8, 16)
DN_ALPHA = float((2 * DEPTH) ** 0.25)
LN_EPS = 1e-5

LANES = 128
ROW_TILE = 512
ROW_BLOCK = 64
CONV_HALO = 32
POOL_HALO = 16
VMEM_LIMIT_BYTES = 56 * 1024 * 1024


def _layer_norm(v, g, b):
    mu = jnp.mean(v, axis=-1, keepdims=True)
    vc = v - mu
    var = jnp.mean(vc * vc, axis=-1, keepdims=True)
    return vc * lax.rsqrt(var + LN_EPS) * g + b


def _bf16_dot(a, b):
    return jnp.dot(a.astype(BF16), b, preferred_element_type=F32)


def _conv_mixer_kernel(x_ref, win_ref, bin_ref, wdw_ref, bdw_ref, lng_ref, lnb_ref,
                       wout_ref, bout_ref, g1_ref, b1_ref, o_ref, glu_ref, conv_ref):
    tm, d = conv_ref.shape
    n_lane_blocks = d // LANES
    i = pl.program_id(1)
    x = x_ref[0]
    h = _bf16_dot(x, win_ref[...]) + bin_ref[...]
    glu = h[:, :d] * jax.nn.sigmoid(h[:, d:])

    @pl.when(i == 0)
    def _():
        glu_ref[:, 0:CONV_HALO, :] = jnp.zeros((n_lane_blocks, CONV_HALO, LANES), F32)

    @pl.when(i > 0)
    def _():
        glu_ref[:, 0:CONV_HALO, :] = glu_ref[:, tm:tm + CONV_HALO, :]

    for c in range(n_lane_blocks):
        glu_ref[c, CONV_HALO:, :] = glu[:, c * LANES:(c + 1) * LANES]

    def row_block(r, carry):
        r0 = pl.multiple_of(r * ROW_BLOCK, ROW_BLOCK)
        for c in range(n_lane_blocks):
            lanes = slice(c * LANES, (c + 1) * LANES)
            acc = None
            for k in range(CONV_WIDTH):
                seg = glu_ref[c, pl.ds(r0 + CONV_HALO - (CONV_WIDTH - 1) + k, ROW_BLOCK), :]
                term = seg * wdw_ref[k:k + 1, lanes]
                acc = term if acc is None else acc + term
            conv_ref[pl.ds(r0, ROW_BLOCK), lanes] = acc
        return carry

    lax.fori_loop(0, tm // ROW_BLOCK, row_block, 0)

    u = _layer_norm(conv_ref[...] + bdw_ref[...], lng_ref[...], lnb_ref[...])
    u = u * jax.nn.sigmoid(u)
    mix = _bf16_dot(u, wout_ref[...]) + bout_ref[...]
    o_ref[0] = _layer_norm(DN_ALPHA * x + mix, g1_ref[...], b1_ref[...])


def _pool_mixer_kernel(x_ref, xprev_ref, pw_ref, pb_ref, ps_ref, g1_ref, b1_ref, o_ref,
                       slab_ref, pooled_ref):
    tm, d = pooled_ref.shape
    n_lane_blocks = d // LANES
    group = d // len(POOL_WINDOWS)
    blocks_per_group = group // LANES
    i = pl.program_id(1)
    x = x_ref[0]
    prev = jnp.where(i > 0, xprev_ref[0], 0.0)
    for c in range(n_lane_blocks):
        lanes = slice(c * LANES, (c + 1) * LANES)
        slab_ref[c, 0:POOL_HALO, :] = prev[:, lanes]
        slab_ref[c, POOL_HALO:, :] = x[:, lanes]

    def row_block(r, carry):
        r0 = pl.multiple_of(r * ROW_BLOCK, ROW_BLOCK)
        row = lax.broadcasted_iota(jnp.int32, (ROW_BLOCK, LANES), 0)
        pos = (i * tm + r0 + row + 1).astype(F32)
        for g, w in enumerate(POOL_WINDOWS):
            div = jnp.minimum(pos, float(w))
            for cc in range(blocks_per_group):
                c = g * blocks_per_group + cc
                cur = slab_ref[c, pl.ds(r0 + POOL_HALO, ROW_BLOCK), :]
                s = cur
                for j in range(1, w):
                    s = s + slab_ref[c, pl.ds(r0 + POOL_HALO - j, ROW_BLOCK), :]
                pooled_ref[pl.ds(r0, ROW_BLOCK), c * LANES:(c + 1) * LANES] = (
                    s / div - cur).astype(BF16)
        return carry

    lax.fori_loop(0, tm // ROW_BLOCK, row_block, 0)

    pooled = pooled_ref[...]
    ys = [jnp.dot(pooled[:, g * group:(g + 1) * group], pw_ref[g], preferred_element_type=F32)
          for g in range(len(POOL_WINDOWS))]
    y = (jnp.concatenate(ys, axis=-1) + pb_ref[...]) * ps_ref[...]
    o_ref[0] = _layer_norm(DN_ALPHA * x + y, g1_ref[...], b1_ref[...])


def _ffn_kernel(x_ref, wg_ref, wu_ref, wd_ref, g2_ref, b2_ref, o_ref):
    x = x_ref[0]
    xb = x.astype(BF16)
    gate = jnp.dot(xb, wg_ref[...], preferred_element_type=F32)
    up = jnp.dot(xb, wu_ref[...], preferred_element_type=F32)
    hidden = gate * jax.nn.sigmoid(gate) * up
    y = _bf16_dot(hidden, wd_ref[...])
    o_ref[0] = _layer_norm(DN_ALPHA * x + y, g2_ref[...], b2_ref[...])


def _resident(shape):
    zeros = (0,) * len(shape)
    return pl.BlockSpec(shape, lambda b, i: zeros, pipeline_mode=pl.Buffered(1))


def _row_tile_spec(tm, d):
    return pl.BlockSpec((1, tm, d), lambda b, i: (b, i, 0))


def _compiler_params():
    return pltpu.CompilerParams(dimension_semantics=("arbitrary", "arbitrary"),
                                vmem_limit_bytes=VMEM_LIMIT_BYTES)


def _row(v):
    return v.reshape(1, -1)


def _conv_mixer(x, w_in, b_in, w_dw, b_dw, ln_g, ln_b, w_out, b_out, g1, b1, tm):
    bsz, s, d = x.shape
    vecs = [_row(b_in), w_dw.reshape(CONV_WIDTH, d), _row(b_dw), _row(ln_g), _row(ln_b)]
    tail = [_row(b_out), _row(g1), _row(b1)]
    args = [x, w_in.astype(BF16)] + vecs + [w_out.astype(BF16)] + tail
    in_specs = [_row_tile_spec(tm, d)] + [_resident(a.shape) for a in args[1:]]
    return pl.pallas_call(
        _conv_mixer_kernel,
        grid=(bsz, s // tm),
        in_specs=in_specs,
        out_specs=_row_tile_spec(tm, d),
        out_shape=jax.ShapeDtypeStruct(x.shape, F32),
        scratch_shapes=[pltpu.VMEM((d // LANES, CONV_HALO + tm, LANES), F32),
                        pltpu.VMEM((tm, d), F32)],
        compiler_params=_compiler_params(),
        name="conv_mixer",
    )(*args)


def _pool_mixer(x, p_w, p_b, p_scale, g1, b1, tm):
    bsz, s, d = x.shape
    halo_blocks_per_tile = tm // POOL_HALO
    prev_spec = pl.BlockSpec(
        (1, POOL_HALO, d), lambda b, i: (b, jnp.maximum(i * halo_blocks_per_tile - 1, 0), 0))
    args = [x, x, p_w.astype(BF16), _row(p_b), _row(p_scale), _row(g1), _row(b1)]
    in_specs = [_row_tile_spec(tm, d), prev_spec] + [_resident(a.shape) for a in args[2:]]
    return pl.pallas_call(
        _pool_mixer_kernel,
        grid=(bsz, s // tm),
        in_specs=in_specs,
        out_specs=_row_tile_spec(tm, d),
        out_shape=jax.ShapeDtypeStruct(x.shape, F32),
        scratch_shapes=[pltpu.VMEM((d // LANES, POOL_HALO + tm, LANES), F32),
                        pltpu.VMEM((tm, d), BF16)],
        compiler_params=_compiler_params(),
        name="pool_mixer",
    )(*args)


def _ffn(x, w_gate, w_up, w_down, g2, b2, tm):
    bsz, s, d = x.shape
    args = [x, w_gate.astype(BF16), w_up.astype(BF16), w_down.astype(BF16), _row(g2), _row(b2)]
    in_specs = [_row_tile_spec(tm, d)] + [_resident(a.shape) for a in args[1:]]
    return pl.pallas_call(
        _ffn_kernel,
        grid=(bsz, s // tm),
        in_specs=in_specs,
        out_specs=_row_tile_spec(tm, d),
        out_shape=jax.ShapeDtypeStruct(x.shape, F32),
        compiler_params=_compiler_params(),
        name="ffn",
    )(*args)


def kernel(x, a_w_in, a_b_in, a_w_dw, a_b_dw, a_ln_g, a_ln_b, a_w_out, a_b_out, p_w, p_b, p_scale, ffn_w_gate, ffn_w_up, ffn_w_down, ln1_g, ln1_b, ln2_g, ln2_b):
    s = x.shape[1]
    tm = math.gcd(s, ROW_TILE)
    assert tm % ROW_BLOCK == 0 and tm >= CONV_HALO, "sequence length must tile by ROW_BLOCK"
    for i in range(DEPTH):
        j = i // 2
        if i % 2 == 0:
            x = _conv_mixer(x, a_w_in[j], a_b_in[j], a_w_dw[j], a_b_dw[j], a_ln_g[j], a_ln_b[j],
                            a_w_out[j], a_b_out[j], ln1_g[i], ln1_b[i], tm)
        else:
            x = _pool_mixer(x, p_w[j], p_b[j], p_scale[j], ln1_g[i], ln1_b[i], tm)
        x = _ffn(x, ffn_w_gate[i], ffn_w_up[i], ffn_w_down[i], ln2_g[i], ln2_b[i], tm)
    return x
```

```python
import functools
import math

import jax
import jax.numpy as jnp
from jax import lax
from jax.experimental import pallas as pl
from jax.experimental.pallas import tpu as pltpu

F32 = jnp.float32
BF16 = jnp.bfloat16

DEPTH = 4
CONV_WIDTH = 31
POOL_WINDOWS = (2, 4, 8, 16)
DN_ALPHA = float((2 * DEPTH) ** 0.25)
LN_EPS = 1e-5

LANES = 128
MXU_COLS = 256
CONV_ROW_TILE = 512
CONV_STAGES = 2
ROW_TILE = 512
ROW_BLOCK = 64
CONV_HALO = 32
POOL_HALO = 16
VMEM_LIMIT_BYTES = 60 * 1024 * 1024


def _layer_norm(v, g, b):
    mu = jnp.mean(v, axis=-1, keepdims=True)
    vc = v - mu
    var = jnp.mean(vc * vc, axis=-1, keepdims=True)
    return vc * lax.rsqrt(var + LN_EPS) * g + b


def _bf16_dot(a, b):
    return jnp.dot(a.astype(BF16), b, preferred_element_type=F32)


def _swiglu_ffn(x, wg_ref, wu_ref, wd_ref, g2_ref, b2_ref):
    xb = x.astype(BF16)
    gate = jnp.dot(xb, wg_ref[...], preferred_element_type=F32)
    up = jnp.dot(xb, wu_ref[...], preferred_element_type=F32)
    hidden = gate * jax.nn.sigmoid(gate) * up
    y = _bf16_dot(hidden, wd_ref[...])
    return _layer_norm(DN_ALPHA * x + y, g2_ref[...], b2_ref[...])


def _chunk(k):
    return slice(k * MXU_COLS, (k + 1) * MXU_COLS)


def _conv_layer_kernel(xa_ref, xd_ref, win_ref, bin_ref, wdw_ref, bdw_ref, lng_ref, lnb_ref,
                       wout_ref, bout_ref, g1_ref, b1_ref, wg_ref, wu_ref, wd_ref, g2_ref, b2_ref,
                       o_ref, glu_ref, conv_ref, u_ref, x1_ref, x1b_ref, hid_ref, *, tiles_per_seq):
    tm, d = conv_ref.shape
    d_ff = hid_ref.shape[1]
    n_lane_blocks = d // LANES
    n_ff = d_ff // MXU_COLS
    s = pl.program_id(0)

    @pl.when(s == 0)
    def _():
        u_ref[...] = jnp.zeros(u_ref.shape, BF16)
        glu_ref[:, tm:, :] = jnp.zeros((n_lane_blocks, CONV_HALO, LANES), F32)

    mix = jnp.dot(u_ref[...], wout_ref[...], preferred_element_type=F32) + bout_ref[...]
    x1 = _layer_norm(DN_ALPHA * xd_ref[...] + mix, g1_ref[...], b1_ref[...])
    x1_ref[...] = x1
    x1b_ref[...] = x1.astype(BF16)

    seq_start = (s % tiles_per_seq) == 0
    tail = glu_ref[:, tm:, :]
    glu_ref[:, 0:CONV_HALO, :] = jnp.where(seq_start, 0.0, tail)
    xa = xa_ref[...].astype(BF16)
    blocks_per_chunk = MXU_COLS // LANES
    for g in range(d // MXU_COLS):
        gate_cols = slice(d + g * MXU_COLS, d + (g + 1) * MXU_COLS)
        val = jnp.dot(xa, win_ref[:, _chunk(g)], preferred_element_type=F32) + bin_ref[:, _chunk(g)]
        gt = jnp.dot(xa, win_ref[:, gate_cols], preferred_element_type=F32) + bin_ref[:, gate_cols]
        glu = val * jax.nn.sigmoid(gt)
        for cc in range(blocks_per_chunk):
            glu_ref[g * blocks_per_chunk + cc, CONV_HALO:, :] = glu[:, cc * LANES:(cc + 1) * LANES]

    n_row_blocks = tm // ROW_BLOCK

    def conv_and_ffn_chunk(r, carry):
        cols = pl.ds(pl.multiple_of(r * MXU_COLS, MXU_COLS), MXU_COLS)
        xb = x1b_ref[...]
        gate = jnp.dot(xb, wg_ref[:, cols], preferred_element_type=F32)
        up = jnp.dot(xb, wu_ref[:, cols], preferred_element_type=F32)
        hid_ref[:, cols] = (gate * jax.nn.sigmoid(gate) * up).astype(BF16)
        r0 = pl.multiple_of(r * ROW_BLOCK, ROW_BLOCK)
        for c in range(n_lane_blocks):
            lanes = slice(c * LANES, (c + 1) * LANES)
            acc = None
            for k in range(CONV_WIDTH):
                seg = glu_ref[c, pl.ds(r0 + CONV_HALO - (CONV_WIDTH - 1) + k, ROW_BLOCK), :]
                term = seg * wdw_ref[k:k + 1, lanes]
                acc = term if acc is None else acc + term
            conv_ref[pl.ds(r0, ROW_BLOCK), lanes] = acc
        return carry

    lax.fori_loop(0, n_row_blocks, conv_and_ffn_chunk, 0)

    xb = x1b_ref[...]
    for j in range(n_row_blocks, n_ff):
        gate = jnp.dot(xb, wg_ref[:, _chunk(j)], preferred_element_type=F32)
        up = jnp.dot(xb, wu_ref[:, _chunk(j)], preferred_element_type=F32)
        hid_ref[:, _chunk(j)] = (gate * jax.nn.sigmoid(gate) * up).astype(BF16)
    u = _layer_norm(conv_ref[...] + bdw_ref[...], lng_ref[...], lnb_ref[...])
    u_ref[...] = (u * jax.nn.sigmoid(u)).astype(BF16)
    y = jnp.dot(hid_ref[...], wd_ref[...], preferred_element_type=F32)
    o_ref[...] = _layer_norm(DN_ALPHA * x1_ref[...] + y, g2_ref[...], b2_ref[...])


def _pool_mixer_kernel(x_ref, xprev_ref, pw_ref, pb_ref, ps_ref, g1_ref, b1_ref, o_ref,
                       slab_ref, pooled_ref):
    tm, d = pooled_ref.shape
    n_lane_blocks = d // LANES
    group = d // len(POOL_WINDOWS)
    blocks_per_group = group // LANES
    i = pl.program_id(1)
    x = x_ref[0]
    prev = jnp.where(i > 0, xprev_ref[0], 0.0)
    for c in range(n_lane_blocks):
        lanes = slice(c * LANES, (c + 1) * LANES)
        slab_ref[c, 0:POOL_HALO, :] = prev[:, lanes]
        slab_ref[c, POOL_HALO:, :] = x[:, lanes]

    def row_block(r, carry):
        r0 = pl.multiple_of(r * ROW_BLOCK, ROW_BLOCK)
        row = lax.broadcasted_iota(jnp.int32, (ROW_BLOCK, LANES), 0)
        pos = (i * tm + r0 + row + 1).astype(F32)
        for g, w in enumerate(POOL_WINDOWS):
            div = jnp.minimum(pos, float(w))
            for cc in range(blocks_per_group):
                c = g * blocks_per_group + cc
                cur = slab_ref[c, pl.ds(r0 + POOL_HALO, ROW_BLOCK), :]
                acc = cur
                for j in range(1, w):
                    acc = acc + slab_ref[c, pl.ds(r0 + POOL_HALO - j, ROW_BLOCK), :]
                pooled_ref[pl.ds(r0, ROW_BLOCK), c * LANES:(c + 1) * LANES] = (
                    acc / div - cur).astype(BF16)
        return carry

    lax.fori_loop(0, tm // ROW_BLOCK, row_block, 0)

    pooled = pooled_ref[...]
    ys = [jnp.dot(pooled[:, g * group:(g + 1) * group], pw_ref[g], preferred_element_type=F32)
          for g in range(len(POOL_WINDOWS))]
    y = (jnp.concatenate(ys, axis=-1) + pb_ref[...]) * ps_ref[...]
    o_ref[0] = _layer_norm(DN_ALPHA * x + y, g1_ref[...], b1_ref[...])


def _ffn_kernel(x_ref, wg_ref, wu_ref, wd_ref, g2_ref, b2_ref, o_ref):
    o_ref[0] = _swiglu_ffn(x_ref[0], wg_ref, wu_ref, wd_ref, g2_ref, b2_ref)


def _layer_slice(stacked, layer, n_grid_axes):
    zeros = (0,) * (stacked.ndim - 1)
    if n_grid_axes == 1:
        index_map = lambda s: (layer,) + zeros
    else:
        index_map = lambda b, i: (layer,) + zeros
    return pl.BlockSpec((None,) + stacked.shape[1:], index_map, pipeline_mode=pl.Buffered(1))


def _rows(v):
    return v.reshape(v.shape[0], 1, -1)


def _delayed_tile_spec(tm, d, delay, n_tiles):
    return pl.BlockSpec((tm, d), lambda s: (jnp.clip(s - delay, 0, n_tiles - 1), 0))


def _conv_layer(x, j, i, conv_params, ffn_params, ln_params, tm):
    bsz, seq, d = x.shape
    n_tiles = bsz * seq // tm
    x2 = x.reshape(bsz * seq, d)
    w_in, b_in, w_dw, b_dw, ln_g, ln_b, w_out, b_out = conv_params
    w_gate, w_up, w_down = ffn_params
    d_ff = w_gate.shape[-1]
    assert d % MXU_COLS == 0 and d_ff % MXU_COLS == 0 and d_ff // MXU_COLS >= tm // ROW_BLOCK
    g1, b1, g2, b2 = ln_params
    mixer_args = [(w_in, j), (b_in, j), (w_dw, j), (b_dw, j), (ln_g, j), (ln_b, j),
                  (w_out, j), (b_out, j), (g1, i), (b1, i)]
    ffn_args = [(w_gate, i), (w_up, i), (w_down, i), (g2, i), (b2, i)]
    params = mixer_args + ffn_args
    out = pl.pallas_call(
        functools.partial(_conv_layer_kernel, tiles_per_seq=seq // tm),
        grid=(n_tiles + CONV_STAGES - 1,),
        in_specs=[_delayed_tile_spec(tm, d, 0, n_tiles), _delayed_tile_spec(tm, d, 1, n_tiles)]
        + [_layer_slice(a, k, 1) for a, k in params],
        out_specs=_delayed_tile_spec(tm, d, CONV_STAGES - 1, n_tiles),
        out_shape=jax.ShapeDtypeStruct(x2.shape, F32),
        scratch_shapes=[pltpu.VMEM((d // LANES, CONV_HALO + tm, LANES), F32),
                        pltpu.VMEM((tm, d), F32),
                        pltpu.VMEM((tm, d), BF16),
                        pltpu.VMEM((tm, d), F32),
                        pltpu.VMEM((tm, d), BF16),
                        pltpu.VMEM((tm, d_ff), BF16)],
        compiler_params=pltpu.CompilerParams(
            dimension_semantics=("arbitrary",), vmem_limit_bytes=VMEM_LIMIT_BYTES),
        name="conv_layer",
    )(x2, x2, *[a for a, _ in params])
    return out.reshape(x.shape)


def _row_tile_spec(tm, d):
    return pl.BlockSpec((1, tm, d), lambda b, i: (b, i, 0))


def _tiled_params():
    return pltpu.CompilerParams(dimension_semantics=("arbitrary", "arbitrary"),
                                vmem_limit_bytes=VMEM_LIMIT_BYTES)


def _pool_mixer(x, j, i, pool_params, ln_params, tm):
    bsz, seq, d = x.shape
    p_w, p_b, p_scale = pool_params
    g1, b1, _, _ = ln_params
    halo_blocks_per_tile = tm // POOL_HALO
    prev_spec = pl.BlockSpec(
        (1, POOL_HALO, d), lambda b, t: (b, jnp.maximum(t * halo_blocks_per_tile - 1, 0), 0))
    params = [(p_w, j), (p_b, j), (p_scale, j), (g1, i), (b1, i)]
    return pl.pallas_call(
        _pool_mixer_kernel,
        grid=(bsz, seq // tm),
        in_specs=[_row_tile_spec(tm, d), prev_spec] + [_layer_slice(a, k, 2) for a, k in params],
        out_specs=_row_tile_spec(tm, d),
        out_shape=jax.ShapeDtypeStruct(x.shape, F32),
        scratch_shapes=[pltpu.VMEM((d // LANES, POOL_HALO + tm, LANES), F32),
                        pltpu.VMEM((tm, d), BF16)],
        compiler_params=_tiled_params(),
        name="pool_mixer",
    )(x, x, *[a for a, _ in params])


def _ffn(x, i, ffn_params, ln_params, tm):
    bsz, seq, d = x.shape
    w_gate, w_up, w_down = ffn_params
    _, _, g2, b2 = ln_params
    params = [(w_gate, i), (w_up, i), (w_down, i), (g2, i), (b2, i)]
    return pl.pallas_call(
        _ffn_kernel,
        grid=(bsz, seq // tm),
        in_specs=[_row_tile_spec(tm, d)] + [_layer_slice(a, k, 2) for a, k in params],
        out_specs=_row_tile_spec(tm, d),
        out_shape=jax.ShapeDtypeStruct(x.shape, F32),
        compiler_params=_tiled_params(),
        name="ffn",
    )(x, *[a for a, _ in params])


def kernel(x, a_w_in, a_b_in, a_w_dw, a_b_dw, a_ln_g, a_ln_b, a_w_out, a_b_out, p_w, p_b, p_scale, ffn_w_gate, ffn_w_up, ffn_w_down, ln1_g, ln1_b, ln2_g, ln2_b):
    seq, d = x.shape[1], x.shape[2]
    tm = math.gcd(seq, ROW_TILE)
    conv_tm = math.gcd(seq, CONV_ROW_TILE)
    assert tm % ROW_BLOCK == 0 and conv_tm % ROW_BLOCK == 0 and conv_tm >= CONV_HALO
    conv_params = (a_w_in.astype(BF16), _rows(a_b_in), a_w_dw.reshape(-1, CONV_WIDTH, d),
                   _rows(a_b_dw), _rows(a_ln_g), _rows(a_ln_b), a_w_out.astype(BF16), _rows(a_b_out))
    pool_params = (p_w.astype(BF16), _rows(p_b), _rows(p_scale))
    ffn_params = (ffn_w_gate.astype(BF16), ffn_w_up.astype(BF16), ffn_w_down.astype(BF16))
    ln_params = (_rows(ln1_g), _rows(ln1_b), _rows(ln2_g), _rows(ln2_b))
    for i in range(DEPTH):
        j = i // 2
        if i % 2 == 0:
            x = _conv_layer(x, j, i, conv_params, ffn_params, ln_params, conv_tm)
        else:
            x = _pool_mixer(x, j, i, pool_params, ln_params, tm)
            x = _ffn(x, i, ffn_params, ln_params, tm)
    return x
```

```python
import math

import jax
import jax.numpy as jnp
from jax import lax
from jax.experimental import pallas as pl
from jax.experimental.pallas import tpu as pltpu

F32 = jnp.float32
BF16 = jnp.bfloat16

DEPTH = 4
CONV_WIDTH = 31
POOL_WINDOWS = (2, 4, 8, 16)
DN_ALPHA = float((2 * DEPTH) ** 0.25)
LN_EPS = 1e-5

LANES = 128
ROW_TILE = 512
ROW_BLOCK = 64
CONV_HALO = 32
POOL_HALO = 16
VMEM_LIMIT_BYTES = 56 * 1024 * 1024


def _layer_norm(v, g, b):
    mu = jnp.mean(v, axis=-1, keepdims=True)
    vc = v - mu
    var = jnp.mean(vc * vc, axis=-1, keepdims=True)
    return vc * lax.rsqrt(var + LN_EPS) * g + b


def _bf16_dot(a, b):
    return jnp.dot(a.astype(BF16), b, preferred_element_type=F32)


def _conv_mixer_kernel(x_ref, win_ref, bin_ref, wdw_ref, bdw_ref, lng_ref, lnb_ref,
                       wout_ref, bout_ref, g1_ref, b1_ref, o_ref, glu_ref, conv_ref):
    tm, d = conv_ref.shape
    n_lane_blocks = d // LANES
    i = pl.program_id(1)
    x = x_ref[0]
    h = _bf16_dot(x, win_ref[...]) + bin_ref[...]
    glu = h[:, :d] * jax.nn.sigmoid(h[:, d:])

    @pl.when(i == 0)
    def _():
        glu_ref[:, 0:CONV_HALO, :] = jnp.zeros((n_lane_blocks, CONV_HALO, LANES), F32)

    @pl.when(i > 0)
    def _():
        glu_ref[:, 0:CONV_HALO, :] = glu_ref[:, tm:tm + CONV_HALO, :]

    for c in range(n_lane_blocks):
        glu_ref[c, CONV_HALO:, :] = glu[:, c * LANES:(c + 1) * LANES]

    def row_block(r, carry):
        r0 = pl.multiple_of(r * ROW_BLOCK, ROW_BLOCK)
        for c in range(n_lane_blocks):
            lanes = slice(c * LANES, (c + 1) * LANES)
            acc = None
            for k in range(CONV_WIDTH):
                seg = glu_ref[c, pl.ds(r0 + CONV_HALO - (CONV_WIDTH - 1) + k, ROW_BLOCK), :]
                term = seg * wdw_ref[k:k + 1, lanes]
                acc = term if acc is None else acc + term
            conv_ref[pl.ds(r0, ROW_BLOCK), lanes] = acc
        return carry

    lax.fori_loop(0, tm // ROW_BLOCK, row_block, 0)

    u = _layer_norm(conv_ref[...] + bdw_ref[...], lng_ref[...], lnb_ref[...])
    u = u * jax.nn.sigmoid(u)
    mix = _bf16_dot(u, wout_ref[...]) + bout_ref[...]
    o_ref[0] = _layer_norm(DN_ALPHA * x + mix, g1_ref[...], b1_ref[...])


def _pool_mixer_kernel(x_ref, xprev_ref, pw_ref, pb_ref, ps_ref, g1_ref, b1_ref, o_ref,
                       slab_ref, pooled_ref):
    tm, d = pooled_ref.shape
    n_lane_blocks = d // LANES
    group = d // len(POOL_WINDOWS)
    blocks_per_group = group // LANES
    i = pl.program_id(1)
    x = x_ref[0]
    prev = jnp.where(i > 0, xprev_ref[0], 0.0)
    for c in range(n_lane_blocks):
        lanes = slice(c * LANES, (c + 1) * LANES)
        slab_ref[c, 0:POOL_HALO, :] = prev[:, lanes]
        slab_ref[c, POOL_HALO:, :] = x[:, lanes]

    def row_block(r, carry):
        r0 = pl.multiple_of(r * ROW_BLOCK, ROW_BLOCK)
        row = lax.broadcasted_iota(jnp.int32, (ROW_BLOCK, LANES), 0)
        pos = (i * tm + r0 + row + 1).astype(F32)
        for g, w in enumerate(POOL_WINDOWS):
            div = jnp.minimum(pos, float(w))
            for cc in range(blocks_per_group):
                c = g * blocks_per_group + cc
                cur = slab_ref[c, pl.ds(r0 + POOL_HALO, ROW_BLOCK), :]
                acc = cur
                for j in range(1, w):
                    acc = acc + slab_ref[c, pl.ds(r0 + POOL_HALO - j, ROW_BLOCK), :]
                pooled_ref[pl.ds(r0, ROW_BLOCK), c * LANES:(c + 1) * LANES] = (
                    acc / div - cur).astype(BF16)
        return carry

    lax.fori_loop(0, tm // ROW_BLOCK, row_block, 0)

    pooled = pooled_ref[...]
    ys = [jnp.dot(pooled[:, g * group:(g + 1) * group], pw_ref[g], preferred_element_type=F32)
          for g in range(len(POOL_WINDOWS))]
    y = (jnp.concatenate(ys, axis=-1) + pb_ref[...]) * ps_ref[...]
    o_ref[0] = _layer_norm(DN_ALPHA * x + y, g1_ref[...], b1_ref[...])


def _ffn_kernel(x_ref, wg_ref, wu_ref, wd_ref, g2_ref, b2_ref, o_ref):
    x = x_ref[0]
    xb = x.astype(BF16)
    gate = jnp.dot(xb, wg_ref[...], preferred_element_type=F32)
    up = jnp.dot(xb, wu_ref[...], preferred_element_type=F32)
    hidden = gate * jax.nn.sigmoid(gate) * up
    y = _bf16_dot(hidden, wd_ref[...])
    o_ref[0] = _layer_norm(DN_ALPHA * x + y, g2_ref[...], b2_ref[...])


def _layer_slice(stacked, layer):
    zeros = (0,) * (stacked.ndim - 1)
    return pl.BlockSpec((None,) + stacked.shape[1:], lambda b, i: (layer,) + zeros,
                        pipeline_mode=pl.Buffered(1))


def _rows(v):
    return v.reshape(v.shape[0], 1, -1)


def _row_tile_spec(tm, d):
    return pl.BlockSpec((1, tm, d), lambda b, i: (b, i, 0))


def _tiled_call(body, name, x, tm, extra_inputs, extra_specs, params, scratch_shapes):
    bsz, seq, d = x.shape
    return pl.pallas_call(
        body,
        grid=(bsz, seq // tm),
        in_specs=[_row_tile_spec(tm, d)] + extra_specs + [_layer_slice(a, k) for a, k in params],
        out_specs=_row_tile_spec(tm, d),
        out_shape=jax.ShapeDtypeStruct(x.shape, F32),
        scratch_shapes=scratch_shapes,
        compiler_params=pltpu.CompilerParams(dimension_semantics=("arbitrary", "arbitrary"),
                                             vmem_limit_bytes=VMEM_LIMIT_BYTES),
        name=name,
    )(x, *extra_inputs, *[a for a, _ in params])


def _conv_mixer(x, j, i, conv_params, ln_params, tm):
    d = x.shape[-1]
    w_in, b_in, w_dw, b_dw, ln_g, ln_b, w_out, b_out = conv_params
    g1, b1, _, _ = ln_params
    params = [(w_in, j), (b_in, j), (w_dw, j), (b_dw, j), (ln_g, j), (ln_b, j),
              (w_out, j), (b_out, j), (g1, i), (b1, i)]
    scratch = [pltpu.VMEM((d // LANES, CONV_HALO + tm, LANES), F32), pltpu.VMEM((tm, d), F32)]
    return _tiled_call(_conv_mixer_kernel, "conv_mixer", x, tm, [], [], params, scratch)


def _pool_mixer(x, j, i, pool_params, ln_params, tm):
    d = x.shape[-1]
    p_w, p_b, p_scale = pool_params
    g1, b1, _, _ = ln_params
    halo_blocks_per_tile = tm // POOL_HALO
    prev_spec = pl.BlockSpec(
        (1, POOL_HALO, d), lambda b, t: (b, jnp.maximum(t * halo_blocks_per_tile - 1, 0), 0))
    params = [(p_w, j), (p_b, j), (p_scale, j), (g1, i), (b1, i)]
    scratch = [pltpu.VMEM((d // LANES, POOL_HALO + tm, LANES), F32), pltpu.VMEM((tm, d), BF16)]
    return _tiled_call(_pool_mixer_kernel, "pool_mixer", x, tm, [x], [prev_spec], params, scratch)


def _ffn(x, i, ffn_params, ln_params, tm):
    w_gate, w_up, w_down = ffn_params
    _, _, g2, b2 = ln_params
    params = [(w_gate, i), (w_up, i), (w_down, i), (g2, i), (b2, i)]
    return _tiled_call(_ffn_kernel, "ffn", x, tm, [], [], params, [])


def kernel(x, a_w_in, a_b_in, a_w_dw, a_b_dw, a_ln_g, a_ln_b, a_w_out, a_b_out, p_w, p_b, p_scale, ffn_w_gate, ffn_w_up, ffn_w_down, ln1_g, ln1_b, ln2_g, ln2_b):
    seq, d = x.shape[1], x.shape[2]
    tm = math.gcd(seq, ROW_TILE)
    assert tm % ROW_BLOCK == 0 and tm >= CONV_HALO
    conv_params = (a_w_in.astype(BF16), _rows(a_b_in), a_w_dw.reshape(-1, CONV_WIDTH, d),
                   _rows(a_b_dw), _rows(a_ln_g), _rows(a_ln_b), a_w_out.astype(BF16), _rows(a_b_out))
    pool_params = (p_w.astype(BF16), _rows(p_b), _rows(p_scale))
    ffn_params = (ffn_w_gate.astype(BF16), ffn_w_up.astype(BF16), ffn_w_down.astype(BF16))
    ln_params = (_rows(ln1_g), _rows(ln1_b), _rows(ln2_g), _rows(ln2_b))
    for i in range(DEPTH):
        j = i // 2
        if i % 2 == 0:
            x = _conv_mixer(x, j, i, conv_params, ln_params, tm)
        else:
            x = _pool_mixer(x, j, i, pool_params, ln_params, tm)
        x = _ffn(x, i, ffn_params, ln_params, tm)
    return x
```

```python
import math

import jax
import jax.numpy as jnp
from jax import lax
from jax.experimental import pallas as pl
from jax.experimental.pallas import tpu as pltpu

F32 = jnp.float32
BF16 = jnp.bfloat16

DEPTH = 4
CONV_WIDTH = 31
POOL_WINDOWS = (2, 4, 8, 16)
DN_ALPHA = float((2 * DEPTH) ** 0.25)
LN_EPS = 1e-5

LANES = 128
ROW_TILE = 1024
FFN_SPLIT = 1536
ROW_BLOCK = 64
CONV_HALO = 32
POOL_HALO = 16
VMEM_LIMIT_BYTES = 60 * 1024 * 1024


def _layer_norm(v, g, b):
    mu = jnp.mean(v, axis=-1, keepdims=True)
    vc = v - mu
    var = jnp.mean(vc * vc, axis=-1, keepdims=True)
    return vc * lax.rsqrt(var + LN_EPS) * g + b


def _bf16_dot(a, b):
    return jnp.dot(a.astype(BF16), b, preferred_element_type=F32)


def _conv_mixer_kernel(x_ref, win_ref, bin_ref, wdw_ref, bdw_ref, lng_ref, lnb_ref,
                       wout_ref, bout_ref, g1_ref, b1_ref, o_ref, glu_ref, conv_ref):
    tm, d = conv_ref.shape
    n_lane_blocks = d // LANES
    i = pl.program_id(1)
    x = x_ref[0]
    h = _bf16_dot(x, win_ref[...]) + bin_ref[...]
    glu = h[:, :d] * jax.nn.sigmoid(h[:, d:])

    @pl.when(i == 0)
    def _():
        glu_ref[:, 0:CONV_HALO, :] = jnp.zeros((n_lane_blocks, CONV_HALO, LANES), F32)

    @pl.when(i > 0)
    def _():
        glu_ref[:, 0:CONV_HALO, :] = glu_ref[:, tm:tm + CONV_HALO, :]

    for c in range(n_lane_blocks):
        glu_ref[c, CONV_HALO:, :] = glu[:, c * LANES:(c + 1) * LANES]

    def row_block(r, carry):
        r0 = pl.multiple_of(r * ROW_BLOCK, ROW_BLOCK)
        for c in range(n_lane_blocks):
            lanes = slice(c * LANES, (c + 1) * LANES)
            acc = None
            for k in range(CONV_WIDTH):
                seg = glu_ref[c, pl.ds(r0 + CONV_HALO - (CONV_WIDTH - 1) + k, ROW_BLOCK), :]
                term = seg * wdw_ref[k:k + 1, lanes]
                acc = term if acc is None else acc + term
            conv_ref[pl.ds(r0, ROW_BLOCK), lanes] = acc
        return carry

    lax.fori_loop(0, tm // ROW_BLOCK, row_block, 0)

    u = _layer_norm(conv_ref[...] + bdw_ref[...], lng_ref[...], lnb_ref[...])
    u = u * jax.nn.sigmoid(u)
    mix = _bf16_dot(u, wout_ref[...]) + bout_ref[...]
    o_ref[0] = _layer_norm(DN_ALPHA * x + mix, g1_ref[...], b1_ref[...])


def _pool_mixer_kernel(x_ref, xprev_ref, pw_ref, pb_ref, ps_ref, g1_ref, b1_ref, o_ref,
                       slab_ref, pooled_ref):
    tm, d = pooled_ref.shape
    n_lane_blocks = d // LANES
    group = d // len(POOL_WINDOWS)
    blocks_per_group = group // LANES
    i = pl.program_id(1)
    x = x_ref[0]
    prev = jnp.where(i > 0, xprev_ref[0], 0.0)
    for c in range(n_lane_blocks):
        lanes = slice(c * LANES, (c + 1) * LANES)
        slab_ref[c, 0:POOL_HALO, :] = prev[:, lanes]
        slab_ref[c, POOL_HALO:, :] = x[:, lanes]

    def row_block(r, carry):
        r0 = pl.multiple_of(r * ROW_BLOCK, ROW_BLOCK)
        row = lax.broadcasted_iota(jnp.int32, (ROW_BLOCK, LANES), 0)
        pos = (i * tm + r0 + row + 1).astype(F32)
        for g, w in enumerate(POOL_WINDOWS):
            div = jnp.minimum(pos, float(w))
            for cc in range(blocks_per_group):
                c = g * blocks_per_group + cc
                cur = slab_ref[c, pl.ds(r0 + POOL_HALO, ROW_BLOCK), :]
                acc = cur
                for j in range(1, w):
                    acc = acc + slab_ref[c, pl.ds(r0 + POOL_HALO - j, ROW_BLOCK), :]
                pooled_ref[pl.ds(r0, ROW_BLOCK), c * LANES:(c + 1) * LANES] = (
                    acc / div - cur).astype(BF16)
        return carry

    lax.fori_loop(0, tm // ROW_BLOCK, row_block, 0)

    pooled = pooled_ref[...]
    ys = [jnp.dot(pooled[:, g * group:(g + 1) * group], pw_ref[g], preferred_element_type=F32)
          for g in range(len(POOL_WINDOWS))]
    y = (jnp.concatenate(ys, axis=-1) + pb_ref[...]) * ps_ref[...]
    o_ref[0] = _layer_norm(DN_ALPHA * x + y, g1_ref[...], b1_ref[...])


def _ffn_kernel(x_ref, wg_ref, wu_ref, wd_ref, g2_ref, b2_ref, o_ref):
    x = x_ref[0]
    xb = x.astype(BF16)
    d_ff = wg_ref.shape[1]
    y = None
    for lo, hi in ((0, FFN_SPLIT), (FFN_SPLIT, d_ff)):
        gate = jnp.dot(xb, wg_ref[:, lo:hi], preferred_element_type=F32)
        up = jnp.dot(xb, wu_ref[:, lo:hi], preferred_element_type=F32)
        hidden = (gate * jax.nn.sigmoid(gate) * up).astype(BF16)
        part = jnp.dot(hidden, wd_ref[lo:hi, :], preferred_element_type=F32)
        y = part if y is None else y + part
    o_ref[0] = _layer_norm(DN_ALPHA * x + y, g2_ref[...], b2_ref[...])


def _layer_slice(stacked, layer):
    zeros = (0,) * (stacked.ndim - 1)
    return pl.BlockSpec((None,) + stacked.shape[1:], lambda b, i: (layer,) + zeros,
                        pipeline_mode=pl.Buffered(1))


def _rows(v):
    return v.reshape(v.shape[0], 1, -1)


def _row_tile_spec(tm, d):
    return pl.BlockSpec((1, tm, d), lambda b, i: (b, i, 0))


def _tiled_call(body, name, x, tm, extra_inputs, extra_specs, params, scratch_shapes):
    bsz, seq, d = x.shape
    return pl.pallas_call(
        body,
        grid=(bsz, seq // tm),
        in_specs=[_row_tile_spec(tm, d)] + extra_specs + [_layer_slice(a, k) for a, k in params],
        out_specs=_row_tile_spec(tm, d),
        out_shape=jax.ShapeDtypeStruct(x.shape, F32),
        scratch_shapes=scratch_shapes,
        compiler_params=pltpu.CompilerParams(dimension_semantics=("arbitrary", "arbitrary"),
                                             vmem_limit_bytes=VMEM_LIMIT_BYTES),
        name=name,
    )(x, *extra_inputs, *[a for a, _ in params])


def _conv_mixer(x, j, i, conv_params, ln_params, tm):
    d = x.shape[-1]
    w_in, b_in, w_dw, b_dw, ln_g, ln_b, w_out, b_out = conv_params
    g1, b1, _, _ = ln_params
    params = [(w_in, j), (b_in, j), (w_dw, j), (b_dw, j), (ln_g, j), (ln_b, j),
              (w_out, j), (b_out, j), (g1, i), (b1, i)]
    scratch = [pltpu.VMEM((d // LANES, CONV_HALO + tm, LANES), F32), pltpu.VMEM((tm, d), F32)]
    return _tiled_call(_conv_mixer_kernel, "conv_mixer", x, tm, [], [], params, scratch)


def _pool_mixer(x, j, i, pool_params, ln_params, tm):
    d = x.shape[-1]
    p_w, p_b, p_scale = pool_params
    g1, b1, _, _ = ln_params
    halo_blocks_per_tile = tm // POOL_HALO
    prev_spec = pl.BlockSpec(
        (1, POOL_HALO, d), lambda b, t: (b, jnp.maximum(t * halo_blocks_per_tile - 1, 0), 0))
    params = [(p_w, j), (p_b, j), (p_scale, j), (g1, i), (b1, i)]
    scratch = [pltpu.VMEM((d // LANES, POOL_HALO + tm, LANES), F32), pltpu.VMEM((tm, d), BF16)]
    return _tiled_call(_pool_mixer_kernel, "pool_mixer", x, tm, [x], [prev_spec], params, scratch)


def _ffn(x, i, ffn_params, ln_params, tm):
    w_gate, w_up, w_down = ffn_params
    assert 0 < FFN_SPLIT < w_gate.shape[-1]
    _, _, g2, b2 = ln_params
    params = [(w_gate, i), (w_up, i), (w_down, i), (g2, i), (b2, i)]
    return _tiled_call(_ffn_kernel, "ffn", x, tm, [], [], params, [])


def kernel(x, a_w_in, a_b_in, a_w_dw, a_b_dw, a_ln_g, a_ln_b, a_w_out, a_b_out, p_w, p_b, p_scale, ffn_w_gate, ffn_w_up, ffn_w_down, ln1_g, ln1_b, ln2_g, ln2_b):
    seq, d = x.shape[1], x.shape[2]
    tm = math.gcd(seq, ROW_TILE)
    assert tm % ROW_BLOCK == 0 and tm >= CONV_HALO
    conv_params = (a_w_in.astype(BF16), _rows(a_b_in), a_w_dw.reshape(-1, CONV_WIDTH, d),
                   _rows(a_b_dw), _rows(a_ln_g), _rows(a_ln_b), a_w_out.astype(BF16), _rows(a_b_out))
    pool_params = (p_w.astype(BF16), _rows(p_b), _rows(p_scale))
    ffn_params = (ffn_w_gate.astype(BF16), ffn_w_up.astype(BF16), ffn_w_down.astype(BF16))
    ln_params = (_rows(ln1_g), _rows(ln1_b), _rows(ln2_g), _rows(ln2_b))
    for i in range(DEPTH):
        j = i // 2
        if i % 2 == 0:
            x = _conv_mixer(x, j, i, conv_params, ln_params, tm)
        else:
            x = _pool_mixer(x, j, i, pool_params, ln_params, tm)
        x = _ffn(x, i, ffn_params, ln_params, tm)
    return x
```

```python
import functools
import math

import jax
import jax.numpy as jnp
from jax import lax
from jax.experimental import pallas as pl
from jax.experimental.pallas import tpu as pltpu

F32 = jnp.float32
BF16 = jnp.bfloat16

DEPTH = 4
CONV_WIDTH = 31
POOL_WINDOWS = (2, 4, 8, 16)
DN_ALPHA = float((2 * DEPTH) ** 0.25)
LN_EPS = 1e-5

LANES = 128
ROW_TILE = 1024
FFN_SPLIT = 1536
WEIGHT_CHUNK_ROWS = 128
ROW_BLOCK = 64
CONV_HALO = 32
POOL_HALO = 16
VMEM_LIMIT_BYTES = 60 * 1024 * 1024


def _layer_norm(v, g, b):
    mu = jnp.mean(v, axis=-1, keepdims=True)
    vc = v - mu
    var = jnp.mean(vc * vc, axis=-1, keepdims=True)
    return vc * lax.rsqrt(var + LN_EPS) * g + b


def _bf16_dot(a, b):
    return jnp.dot(a.astype(BF16), b, preferred_element_type=F32)


def _load_weight_as_bf16(w_hbm, layer, dst_ref, stage_ref, sem_ref):
    n_rows, n_cols = dst_ref.shape
    n_chunks = n_rows // WEIGHT_CHUNK_ROWS

    def copy(c):
        src = w_hbm.at[layer, pl.ds(c * WEIGHT_CHUNK_ROWS, WEIGHT_CHUNK_ROWS), :]
        dst = stage_ref.at[c % 2, :, pl.ds(0, n_cols)]
        return pltpu.make_async_copy(src, dst, sem_ref.at[c % 2])

    copy(0).start()
    for c in range(n_chunks):
        if c + 1 < n_chunks:
            copy(c + 1).start()
        copy(c).wait()
        rows = slice(c * WEIGHT_CHUNK_ROWS, (c + 1) * WEIGHT_CHUNK_ROWS)
        dst_ref[rows, :] = stage_ref[c % 2, :, 0:n_cols].astype(BF16)


def _is_first_step():
    return (pl.program_id(0) == 0) & (pl.program_id(1) == 0)


def _conv_mixer_kernel(x_ref, win_hbm, wout_hbm, bin_ref, wdw_ref, bdw_ref, lng_ref, lnb_ref,
                       bout_ref, g1_ref, b1_ref, o_ref, glu_ref, conv_ref, win_ref, wout_ref,
                       stage_ref, sem_ref, *, layer):
    tm, d = conv_ref.shape
    n_lane_blocks = d // LANES
    i = pl.program_id(1)

    @pl.when(_is_first_step())
    def _():
        _load_weight_as_bf16(win_hbm, layer, win_ref, stage_ref, sem_ref)
        _load_weight_as_bf16(wout_hbm, layer, wout_ref, stage_ref, sem_ref)

    x = x_ref[0]
    h = _bf16_dot(x, win_ref[...]) + bin_ref[...]
    glu = h[:, :d] * jax.nn.sigmoid(h[:, d:])

    @pl.when(i == 0)
    def _():
        glu_ref[:, 0:CONV_HALO, :] = jnp.zeros((n_lane_blocks, CONV_HALO, LANES), F32)

    @pl.when(i > 0)
    def _():
        glu_ref[:, 0:CONV_HALO, :] = glu_ref[:, tm:tm + CONV_HALO, :]

    for c in range(n_lane_blocks):
        glu_ref[c, CONV_HALO:, :] = glu[:, c * LANES:(c + 1) * LANES]

    def row_block(r, carry):
        r0 = pl.multiple_of(r * ROW_BLOCK, ROW_BLOCK)
        for c in range(n_lane_blocks):
            lanes = slice(c * LANES, (c + 1) * LANES)
            acc = None
            for k in range(CONV_WIDTH):
                seg = glu_ref[c, pl.ds(r0 + CONV_HALO - (CONV_WIDTH - 1) + k, ROW_BLOCK), :]
                term = seg * wdw_ref[k:k + 1, lanes]
                acc = term if acc is None else acc + term
            conv_ref[pl.ds(r0, ROW_BLOCK), lanes] = acc
        return carry

    lax.fori_loop(0, tm // ROW_BLOCK, row_block, 0)

    u = _layer_norm(conv_ref[...] + bdw_ref[...], lng_ref[...], lnb_ref[...])
    u = u * jax.nn.sigmoid(u)
    mix = _bf16_dot(u, wout_ref[...]) + bout_ref[...]
    o_ref[0] = _layer_norm(DN_ALPHA * x + mix, g1_ref[...], b1_ref[...])


def _pool_mixer_kernel(x_ref, xprev_ref, pw_ref, pb_ref, ps_ref, g1_ref, b1_ref, o_ref,
                       slab_ref, pooled_ref):
    tm, d = pooled_ref.shape
    n_lane_blocks = d // LANES
    group = d // len(POOL_WINDOWS)
    blocks_per_group = group // LANES
    i = pl.program_id(1)
    x = x_ref[0]
    prev = jnp.where(i > 0, xprev_ref[0], 0.0)
    for c in range(n_lane_blocks):
        lanes = slice(c * LANES, (c + 1) * LANES)
        slab_ref[c, 0:POOL_HALO, :] = prev[:, lanes]
        slab_ref[c, POOL_HALO:, :] = x[:, lanes]

    def row_block(r, carry):
        r0 = pl.multiple_of(r * ROW_BLOCK, ROW_BLOCK)
        row = lax.broadcasted_iota(jnp.int32, (ROW_BLOCK, LANES), 0)
        pos = (i * tm + r0 + row + 1).astype(F32)
        for g, w in enumerate(POOL_WINDOWS):
            div = jnp.minimum(pos, float(w))
            for cc in range(blocks_per_group):
                c = g * blocks_per_group + cc
                cur = slab_ref[c, pl.ds(r0 + POOL_HALO, ROW_BLOCK), :]
                acc = cur
                for j in range(1, w):
                    acc = acc + slab_ref[c, pl.ds(r0 + POOL_HALO - j, ROW_BLOCK), :]
                pooled_ref[pl.ds(r0, ROW_BLOCK), c * LANES:(c + 1) * LANES] = (
                    acc / div - cur).astype(BF16)
        return carry

    lax.fori_loop(0, tm // ROW_BLOCK, row_block, 0)

    pooled = pooled_ref[...]
    ys = [jnp.dot(pooled[:, g * group:(g + 1) * group], pw_ref[g], preferred_element_type=F32)
          for g in range(len(POOL_WINDOWS))]
    y = (jnp.concatenate(ys, axis=-1) + pb_ref[...]) * ps_ref[...]
    o_ref[0] = _layer_norm(DN_ALPHA * x + y, g1_ref[...], b1_ref[...])


def _ffn_kernel(x_ref, wg_hbm, wu_hbm, wd_hbm, g2_ref, b2_ref, o_ref,
                wg_ref, wu_ref, wd_ref, stage_ref, sem_ref, *, layer):
    @pl.when(_is_first_step())
    def _():
        _load_weight_as_bf16(wg_hbm, layer, wg_ref, stage_ref, sem_ref)
        _load_weight_as_bf16(wu_hbm, layer, wu_ref, stage_ref, sem_ref)
        _load_weight_as_bf16(wd_hbm, layer, wd_ref, stage_ref, sem_ref)

    x = x_ref[0]
    xb = x.astype(BF16)
    d_ff = wg_ref.shape[1]
    y = None
    for lo, hi in ((0, FFN_SPLIT), (FFN_SPLIT, d_ff)):
        gate = jnp.dot(xb, wg_ref[:, lo:hi], preferred_element_type=F32)
        up = jnp.dot(xb, wu_ref[:, lo:hi], preferred_element_type=F32)
        hidden = (gate * jax.nn.sigmoid(gate) * up).astype(BF16)
        part = jnp.dot(hidden, wd_ref[lo:hi, :], preferred_element_type=F32)
        y = part if y is None else y + part
    o_ref[0] = _layer_norm(DN_ALPHA * x + y, g2_ref[...], b2_ref[...])


def _layer_slice(stacked, layer):
    zeros = (0,) * (stacked.ndim - 1)
    return pl.BlockSpec((None,) + stacked.shape[1:], lambda b, i: (layer,) + zeros,
                        pipeline_mode=pl.Buffered(1))


def _rows(v):
    return v.reshape(v.shape[0], 1, -1)


_HBM = pl.BlockSpec(memory_space=pl.ANY)


def _row_tile_spec(tm, d):
    return pl.BlockSpec((1, tm, d), lambda b, i: (b, i, 0))


def _weight_staging(n_cols):
    return [pltpu.VMEM((2, WEIGHT_CHUNK_ROWS, n_cols), F32), pltpu.SemaphoreType.DMA((2,))]


def _tiled_call(body, name, x, tm, extra_inputs, extra_specs, params, scratch_shapes):
    bsz, seq, d = x.shape
    return pl.pallas_call(
        body,
        grid=(bsz, seq // tm),
        in_specs=[_row_tile_spec(tm, d)] + extra_specs + [_layer_slice(a, k) for a, k in params],
        out_specs=_row_tile_spec(tm, d),
        out_shape=jax.ShapeDtypeStruct(x.shape, F32),
        scratch_shapes=scratch_shapes,
        compiler_params=pltpu.CompilerParams(dimension_semantics=("arbitrary", "arbitrary"),
                                             vmem_limit_bytes=VMEM_LIMIT_BYTES),
        name=name,
    )(x, *extra_inputs, *[a for a, _ in params])


def _conv_mixer(x, j, i, conv_params, ln_params, tm):
    d = x.shape[-1]
    w_in, b_in, w_dw, b_dw, ln_g, ln_b, w_out, b_out = conv_params
    assert d % WEIGHT_CHUNK_ROWS == 0
    g1, b1, _, _ = ln_params
    params = [(b_in, j), (w_dw, j), (b_dw, j), (ln_g, j), (ln_b, j), (b_out, j), (g1, i), (b1, i)]
    scratch = [pltpu.VMEM((d // LANES, CONV_HALO + tm, LANES), F32), pltpu.VMEM((tm, d), F32),
               pltpu.VMEM((d, 2 * d), BF16), pltpu.VMEM((d, d), BF16)] + _weight_staging(2 * d)
    return _tiled_call(functools.partial(_conv_mixer_kernel, layer=j), "conv_mixer", x, tm,
                       [w_in, w_out], [_HBM] * 2, params, scratch)


def _pool_mixer(x, j, i, pool_params, ln_params, tm):
    d = x.shape[-1]
    p_w, p_b, p_scale = pool_params
    g1, b1, _, _ = ln_params
    halo_blocks_per_tile = tm // POOL_HALO
    prev_spec = pl.BlockSpec(
        (1, POOL_HALO, d), lambda b, t: (b, jnp.maximum(t * halo_blocks_per_tile - 1, 0), 0))
    params = [(p_w, j), (p_b, j), (p_scale, j), (g1, i), (b1, i)]
    scratch = [pltpu.VMEM((d // LANES, POOL_HALO + tm, LANES), F32), pltpu.VMEM((tm, d), BF16)]
    return _tiled_call(_pool_mixer_kernel, "pool_mixer", x, tm, [x], [prev_spec], params, scratch)


def _ffn(x, i, ffn_params, ln_params, tm):
    d = x.shape[-1]
    w_gate, w_up, w_down = ffn_params
    d_ff = w_gate.shape[-1]
    assert 0 < FFN_SPLIT < d_ff and d % WEIGHT_CHUNK_ROWS == 0 and d_ff % WEIGHT_CHUNK_ROWS == 0
    _, _, g2, b2 = ln_params
    scratch = [pltpu.VMEM((d, d_ff), BF16), pltpu.VMEM((d, d_ff), BF16),
               pltpu.VMEM((d_ff, d), BF16)] + _weight_staging(d_ff)
    return _tiled_call(functools.partial(_ffn_kernel, layer=i), "ffn", x, tm,
                       [w_gate, w_up, w_down], [_HBM] * 3, [(g2, i), (b2, i)], scratch)


def kernel(x, a_w_in, a_b_in, a_w_dw, a_b_dw, a_ln_g, a_ln_b, a_w_out, a_b_out, p_w, p_b, p_scale, ffn_w_gate, ffn_w_up, ffn_w_down, ln1_g, ln1_b, ln2_g, ln2_b):
    seq, d = x.shape[1], x.shape[2]
    tm = math.gcd(seq, ROW_TILE)
    assert tm % ROW_BLOCK == 0 and tm >= CONV_HALO
    conv_params = (a_w_in, _rows(a_b_in), a_w_dw.reshape(-1, CONV_WIDTH, d),
                   _rows(a_b_dw), _rows(a_ln_g), _rows(a_ln_b), a_w_out, _rows(a_b_out))
    pool_params = (p_w.astype(BF16), _rows(p_b), _rows(p_scale))
    ffn_params = (ffn_w_gate, ffn_w_up, ffn_w_down)
    ln_params = (_rows(ln1_g), _rows(ln1_b), _rows(ln2_g), _rows(ln2_b))
    for i in range(DEPTH):
        j = i // 2
        if i % 2 == 0:
            x = _conv_mixer(x, j, i, conv_params, ln_params, tm)
        else:
            x = _pool_mixer(x, j, i, pool_params, ln_params, tm)
        x = _ffn(x, i, ffn_params, ln_params, tm)
    return x
```

```python
import functools
import math

import jax
import jax.numpy as jnp
from jax import lax
from jax.experimental import pallas as pl
from jax.experimental.pallas import tpu as pltpu

F32 = jnp.float32
BF16 = jnp.bfloat16

DEPTH = 4
CONV_WIDTH = 31
POOL_WINDOWS = (2, 4, 8, 16)
DN_ALPHA = float((2 * DEPTH) ** 0.25)
LN_EPS = 1e-5

LANES = 128
ROW_TILE = 1024
FFN_SPLIT = 1536
WEIGHT_CHUNK_ROWS = 128
WEIGHT_SLOTS = 6
ROW_BLOCK = 64
CONV_HALO = 32
POOL_HALO = 16
VMEM_LIMIT_BYTES = 60 * 1024 * 1024


def _layer_norm(v, g, b):
    mu = jnp.mean(v, axis=-1, keepdims=True)
    vc = v - mu
    var = jnp.mean(vc * vc, axis=-1, keepdims=True)
    return vc * lax.rsqrt(var + LN_EPS) * g + b


def _bf16_dot(a, b):
    return jnp.dot(a.astype(BF16), b, preferred_element_type=F32)


def _load_weights_as_bf16(layer, weights, stage_ref, sem_ref):
    chunks = [(w_hbm, dst_ref, r0) for w_hbm, dst_ref in weights
              for r0 in range(0, dst_ref.shape[0], WEIGHT_CHUNK_ROWS)]

    def copy(n):
        w_hbm, dst_ref, r0 = chunks[n]
        slot = n % WEIGHT_SLOTS
        src = w_hbm.at[layer, pl.ds(r0, WEIGHT_CHUNK_ROWS), :]
        dst = stage_ref.at[slot, :, pl.ds(0, dst_ref.shape[1])]
        return pltpu.make_async_copy(src, dst, sem_ref.at[slot])

    for n in range(min(WEIGHT_SLOTS - 1, len(chunks))):
        copy(n).start()
    for n, (_, dst_ref, r0) in enumerate(chunks):
        ahead = n + WEIGHT_SLOTS - 1
        if ahead < len(chunks):
            copy(ahead).start()
        copy(n).wait()
        staged = stage_ref[n % WEIGHT_SLOTS, :, 0:dst_ref.shape[1]]
        dst_ref[r0:r0 + WEIGHT_CHUNK_ROWS, :] = staged.astype(BF16)


def _is_first_step():
    return (pl.program_id(0) == 0) & (pl.program_id(1) == 0)


def _conv_mixer_kernel(x_ref, win_hbm, wout_hbm, bin_ref, wdw_ref, bdw_ref, lng_ref, lnb_ref,
                       bout_ref, g1_ref, b1_ref, o_ref, glu_ref, conv_ref, win_ref, wout_ref,
                       stage_ref, sem_ref, *, layer):
    tm, d = conv_ref.shape
    n_lane_blocks = d // LANES
    i = pl.program_id(1)

    @pl.when(_is_first_step())
    def _():
        _load_weights_as_bf16(layer, [(win_hbm, win_ref), (wout_hbm, wout_ref)],
                              stage_ref, sem_ref)

    x = x_ref[0]
    h = _bf16_dot(x, win_ref[...]) + bin_ref[...]
    glu = h[:, :d] * jax.nn.sigmoid(h[:, d:])

    @pl.when(i == 0)
    def _():
        glu_ref[:, 0:CONV_HALO, :] = jnp.zeros((n_lane_blocks, CONV_HALO, LANES), F32)

    @pl.when(i > 0)
    def _():
        glu_ref[:, 0:CONV_HALO, :] = glu_ref[:, tm:tm + CONV_HALO, :]

    for c in range(n_lane_blocks):
        glu_ref[c, CONV_HALO:, :] = glu[:, c * LANES:(c + 1) * LANES]

    def row_block(r, carry):
        r0 = pl.multiple_of(r * ROW_BLOCK, ROW_BLOCK)
        for c in range(n_lane_blocks):
            lanes = slice(c * LANES, (c + 1) * LANES)
            acc = None
            for k in range(CONV_WIDTH):
                seg = glu_ref[c, pl.ds(r0 + CONV_HALO - (CONV_WIDTH - 1) + k, ROW_BLOCK), :]
                term = seg * wdw_ref[k:k + 1, lanes]
                acc = term if acc is None else acc + term
            conv_ref[pl.ds(r0, ROW_BLOCK), lanes] = acc
        return carry

    lax.fori_loop(0, tm // ROW_BLOCK, row_block, 0)

    u = _layer_norm(conv_ref[...] + bdw_ref[...], lng_ref[...], lnb_ref[...])
    u = u * jax.nn.sigmoid(u)
    mix = _bf16_dot(u, wout_ref[...]) + bout_ref[...]
    o_ref[0] = _layer_norm(DN_ALPHA * x + mix, g1_ref[...], b1_ref[...])


def _pool_mixer_kernel(x_ref, xprev_ref, pw_ref, pb_ref, ps_ref, g1_ref, b1_ref, o_ref,
                       slab_ref, pooled_ref):
    tm, d = pooled_ref.shape
    n_lane_blocks = d // LANES
    group = d // len(POOL_WINDOWS)
    blocks_per_group = group // LANES
    i = pl.program_id(1)
    x = x_ref[0]
    prev = jnp.where(i > 0, xprev_ref[0], 0.0)
    for c in range(n_lane_blocks):
        lanes = slice(c * LANES, (c + 1) * LANES)
        slab_ref[c, 0:POOL_HALO, :] = prev[:, lanes]
        slab_ref[c, POOL_HALO:, :] = x[:, lanes]

    def row_block(r, carry):
        r0 = pl.multiple_of(r * ROW_BLOCK, ROW_BLOCK)
        row = lax.broadcasted_iota(jnp.int32, (ROW_BLOCK, LANES), 0)
        pos = (i * tm + r0 + row + 1).astype(F32)
        for g, w in enumerate(POOL_WINDOWS):
            div = jnp.minimum(pos, float(w))
            for cc in range(blocks_per_group):
                c = g * blocks_per_group + cc
                cur = slab_ref[c, pl.ds(r0 + POOL_HALO, ROW_BLOCK), :]
                acc = cur
                for j in range(1, w):
                    acc = acc + slab_ref[c, pl.ds(r0 + POOL_HALO - j, ROW_BLOCK), :]
                pooled_ref[pl.ds(r0, ROW_BLOCK), c * LANES:(c + 1) * LANES] = (
                    acc / div - cur).astype(BF16)
        return carry

    lax.fori_loop(0, tm // ROW_BLOCK, row_block, 0)

    pooled = pooled_ref[...]
    ys = [jnp.dot(pooled[:, g * group:(g + 1) * group], pw_ref[g], preferred_element_type=F32)
          for g in range(len(POOL_WINDOWS))]
    y = (jnp.concatenate(ys, axis=-1) + pb_ref[...]) * ps_ref[...]
    o_ref[0] = _layer_norm(DN_ALPHA * x + y, g1_ref[...], b1_ref[...])


def _ffn_kernel(x_ref, wg_hbm, wu_hbm, wd_hbm, g2_ref, b2_ref, o_ref,
                wg_ref, wu_ref, wd_ref, stage_ref, sem_ref, *, layer):
    @pl.when(_is_first_step())
    def _():
        _load_weights_as_bf16(layer, [(wg_hbm, wg_ref), (wu_hbm, wu_ref), (wd_hbm, wd_ref)],
                              stage_ref, sem_ref)

    x = x_ref[0]
    xb = x.astype(BF16)
    d_ff = wg_ref.shape[1]
    y = None
    for lo, hi in ((0, FFN_SPLIT), (FFN_SPLIT, d_ff)):
        gate = jnp.dot(xb, wg_ref[:, lo:hi], preferred_element_type=F32)
        up = jnp.dot(xb, wu_ref[:, lo:hi], preferred_element_type=F32)
        hidden = (gate * jax.nn.sigmoid(gate) * up).astype(BF16)
        part = jnp.dot(hidden, wd_ref[lo:hi, :], preferred_element_type=F32)
        y = part if y is None else y + part
    o_ref[0] = _layer_norm(DN_ALPHA * x + y, g2_ref[...], b2_ref[...])


def _layer_slice(stacked, layer):
    zeros = (0,) * (stacked.ndim - 1)
    return pl.BlockSpec((None,) + stacked.shape[1:], lambda b, i: (layer,) + zeros,
                        pipeline_mode=pl.Buffered(1))


def _rows(v):
    return v.reshape(v.shape[0], 1, -1)


_HBM = pl.BlockSpec(memory_space=pl.ANY)


def _row_tile_spec(tm, d):
    return pl.BlockSpec((1, tm, d), lambda b, i: (b, i, 0))


def _weight_staging(n_cols):
    return [pltpu.VMEM((WEIGHT_SLOTS, WEIGHT_CHUNK_ROWS, n_cols), F32),
            pltpu.SemaphoreType.DMA((WEIGHT_SLOTS,))]


def _tiled_call(body, name, x, tm, extra_inputs, extra_specs, params, scratch_shapes):
    bsz, seq, d = x.shape
    return pl.pallas_call(
        body,
        grid=(bsz, seq // tm),
        in_specs=[_row_tile_spec(tm, d)] + extra_specs + [_layer_slice(a, k) for a, k in params],
        out_specs=_row_tile_spec(tm, d),
        out_shape=jax.ShapeDtypeStruct(x.shape, F32),
        scratch_shapes=scratch_shapes,
        compiler_params=pltpu.CompilerParams(dimension_semantics=("arbitrary", "arbitrary"),
                                             vmem_limit_bytes=VMEM_LIMIT_BYTES),
        name=name,
    )(x, *extra_inputs, *[a for a, _ in params])


def _conv_mixer(x, j, i, conv_params, ln_params, tm):
    d = x.shape[-1]
    w_in, b_in, w_dw, b_dw, ln_g, ln_b, w_out, b_out = conv_params
    assert d % WEIGHT_CHUNK_ROWS == 0
    g1, b1, _, _ = ln_params
    params = [(b_in, j), (w_dw, j), (b_dw, j), (ln_g, j), (ln_b, j), (b_out, j), (g1, i), (b1, i)]
    scratch = [pltpu.VMEM((d // LANES, CONV_HALO + tm, LANES), F32), pltpu.VMEM((tm, d), F32),
               pltpu.VMEM((d, 2 * d), BF16), pltpu.VMEM((d, d), BF16)] + _weight_staging(2 * d)
    return _tiled_call(functools.partial(_conv_mixer_kernel, layer=j), "conv_mixer", x, tm,
                       [w_in, w_out], [_HBM] * 2, params, scratch)


def _pool_mixer(x, j, i, pool_params, ln_params, tm):
    d = x.shape[-1]
    p_w, p_b, p_scale = pool_params
    g1, b1, _, _ = ln_params
    halo_blocks_per_tile = tm // POOL_HALO
    prev_spec = pl.BlockSpec(
        (1, POOL_HALO, d), lambda b, t: (b, jnp.maximum(t * halo_blocks_per_tile - 1, 0), 0))
    params = [(p_w, j), (p_b, j), (p_scale, j), (g1, i), (b1, i)]
    scratch = [pltpu.VMEM((d // LANES, POOL_HALO + tm, LANES), F32), pltpu.VMEM((tm, d), BF16)]
    return _tiled_call(_pool_mixer_kernel, "pool_mixer", x, tm, [x], [prev_spec], params, scratch)


def _ffn(x, i, ffn_params, ln_params, tm):
    d = x.shape[-1]
    w_gate, w_up, w_down = ffn_params
    d_ff = w_gate.shape[-1]
    assert 0 < FFN_SPLIT < d_ff and d % WEIGHT_CHUNK_ROWS == 0 and d_ff % WEIGHT_CHUNK_ROWS == 0
    _, _, g2, b2 = ln_params
    scratch = [pltpu.VMEM((d, d_ff), BF16), pltpu.VMEM((d, d_ff), BF16),
               pltpu.VMEM((d_ff, d), BF16)] + _weight_staging(d_ff)
    return _tiled_call(functools.partial(_ffn_kernel, layer=i), "ffn", x, tm,
                       [w_gate, w_up, w_down], [_HBM] * 3, [(g2, i), (b2, i)], scratch)


def kernel(x, a_w_in, a_b_in, a_w_dw, a_b_dw, a_ln_g, a_ln_b, a_w_out, a_b_out, p_w, p_b, p_scale, ffn_w_gate, ffn_w_up, ffn_w_down, ln1_g, ln1_b, ln2_g, ln2_b):
    seq, d = x.shape[1], x.shape[2]
    tm = math.gcd(seq, ROW_TILE)
    assert tm % ROW_BLOCK == 0 and tm >= CONV_HALO
    conv_params = (a_w_in, _rows(a_b_in), a_w_dw.reshape(-1, CONV_WIDTH, d),
                   _rows(a_b_dw), _rows(a_ln_g), _rows(a_ln_b), a_w_out, _rows(a_b_out))
    pool_params = (p_w.astype(BF16), _rows(p_b), _rows(p_scale))
    ffn_params = (ffn_w_gate, ffn_w_up, ffn_w_down)
    ln_params = (_rows(ln1_g), _rows(ln1_b), _rows(ln2_g), _rows(ln2_b))
    for i in range(DEPTH):
        j = i // 2
        if i % 2 == 0:
            x = _conv_mixer(x, j, i, conv_params, ln_params, tm)
        else:
            x = _pool_mixer(x, j, i, pool_params, ln_params, tm)
        x = _ffn(x, i, ffn_params, ln_params, tm)
    return x
```

```python
import functools
import math

import jax
import jax.numpy as jnp
from jax import lax
from jax.experimental import pallas as pl
from jax.experimental.pallas import tpu as pltpu

F32 = jnp.float32
BF16 = jnp.bfloat16

DEPTH = 4
CONV_WIDTH = 31
POOL_WINDOWS = (2, 4, 8, 16)
DN_ALPHA = float((2 * DEPTH) ** 0.25)
LN_EPS = 1e-5

LANES = 128
ROW_TILE = 1024
FFN_SPLIT = 1536
WEIGHT_CHUNK_ROWS = 128
WEIGHT_SLOTS = 6
CONV_GROUP = 256
OUT_ROWS = 512
ROW_BLOCK = 64
CONV_HALO = 32
POOL_HALO = 16
VMEM_LIMIT_BYTES = 60 * 1024 * 1024


def _layer_norm(v, g, b):
    mu = jnp.mean(v, axis=-1, keepdims=True)
    vc = v - mu
    var = jnp.mean(vc * vc, axis=-1, keepdims=True)
    return vc * lax.rsqrt(var + LN_EPS) * g + b


def _bf16_dot(a, b):
    return jnp.dot(a.astype(BF16), b, preferred_element_type=F32)


def _load_weights_as_bf16(layer, weights, stage_ref, sem_ref):
    chunks = [(w_hbm, dst_ref, r0) for w_hbm, dst_ref in weights
              for r0 in range(0, dst_ref.shape[0], WEIGHT_CHUNK_ROWS)]

    def copy(n):
        w_hbm, dst_ref, r0 = chunks[n]
        slot = n % WEIGHT_SLOTS
        src = w_hbm.at[layer, pl.ds(r0, WEIGHT_CHUNK_ROWS), :]
        dst = stage_ref.at[slot, :, pl.ds(0, dst_ref.shape[1])]
        return pltpu.make_async_copy(src, dst, sem_ref.at[slot])

    for n in range(min(WEIGHT_SLOTS - 1, len(chunks))):
        copy(n).start()
    for n, (_, dst_ref, r0) in enumerate(chunks):
        ahead = n + WEIGHT_SLOTS - 1
        if ahead < len(chunks):
            copy(ahead).start()
        copy(n).wait()
        staged = stage_ref[n % WEIGHT_SLOTS, :, 0:dst_ref.shape[1]]
        dst_ref[r0:r0 + WEIGHT_CHUNK_ROWS, :] = staged.astype(BF16)


def _is_first_step():
    return (pl.program_id(0) == 0) & (pl.program_id(1) == 0)


def _conv_mixer_kernel(x_ref, win_hbm, wout_hbm, bin_ref, wdw_ref, bdw_ref, lng_ref, lnb_ref,
                       bout_ref, g1_ref, b1_ref, o_ref, glu_ref, conv_ref, win_ref, wout_ref,
                       stage_ref, sem_ref, *, layer):
    tm, d = conv_ref.shape
    n_lane_blocks = d // LANES
    i = pl.program_id(1)

    @pl.when(_is_first_step())
    def _():
        _load_weights_as_bf16(layer, [(win_hbm, win_ref), (wout_hbm, wout_ref)],
                              stage_ref, sem_ref)

    @pl.when(i == 0)
    def _():
        glu_ref[:, 0:CONV_HALO, :] = jnp.zeros((n_lane_blocks, CONV_HALO, LANES), F32)

    @pl.when(i > 0)
    def _():
        glu_ref[:, 0:CONV_HALO, :] = glu_ref[:, tm:tm + CONV_HALO, :]

    xb = x_ref[0].astype(BF16)
    blocks_per_group = CONV_GROUP // LANES
    for g in range(d // CONV_GROUP):
        cols = slice(g * CONV_GROUP, (g + 1) * CONV_GROUP)
        gate_cols = slice(d + g * CONV_GROUP, d + (g + 1) * CONV_GROUP)
        val = jnp.dot(xb, win_ref[:, cols], preferred_element_type=F32) + bin_ref[:, cols]
        gt = jnp.dot(xb, win_ref[:, gate_cols], preferred_element_type=F32) + bin_ref[:, gate_cols]
        glu = val * jax.nn.sigmoid(gt)
        for cc in range(blocks_per_group):
            glu_ref[g * blocks_per_group + cc, CONV_HALO:, :] = glu[:, cc * LANES:(cc + 1) * LANES]

    def row_block(r, carry):
        r0 = pl.multiple_of(r * ROW_BLOCK, ROW_BLOCK)
        for c in range(n_lane_blocks):
            lanes = slice(c * LANES, (c + 1) * LANES)
            acc = None
            for k in range(CONV_WIDTH):
                seg = glu_ref[c, pl.ds(r0 + CONV_HALO - (CONV_WIDTH - 1) + k, ROW_BLOCK), :]
                term = seg * wdw_ref[k:k + 1, lanes]
                acc = term if acc is None else acc + term
            conv_ref[pl.ds(r0, ROW_BLOCK), lanes] = acc
        return carry

    lax.fori_loop(0, tm // ROW_BLOCK, row_block, 0)

    for r0 in range(0, tm, OUT_ROWS):
        rows = slice(r0, r0 + OUT_ROWS)
        u = _layer_norm(conv_ref[rows, :] + bdw_ref[...], lng_ref[...], lnb_ref[...])
        u = u * jax.nn.sigmoid(u)
        mix = _bf16_dot(u, wout_ref[...]) + bout_ref[...]
        o_ref[0, rows, :] = _layer_norm(DN_ALPHA * x_ref[0, rows, :] + mix, g1_ref[...], b1_ref[...])


def _pool_mixer_kernel(x_ref, xprev_ref, pw_ref, pb_ref, ps_ref, g1_ref, b1_ref, o_ref,
                       slab_ref, pooled_ref):
    tm, d = pooled_ref.shape
    n_lane_blocks = d // LANES
    group = d // len(POOL_WINDOWS)
    blocks_per_group = group // LANES
    i = pl.program_id(1)
    x = x_ref[0]
    prev = jnp.where(i > 0, xprev_ref[0], 0.0)
    for c in range(n_lane_blocks):
        lanes = slice(c * LANES, (c + 1) * LANES)
        slab_ref[c, 0:POOL_HALO, :] = prev[:, lanes]
        slab_ref[c, POOL_HALO:, :] = x[:, lanes]

    def row_block(r, carry):
        r0 = pl.multiple_of(r * ROW_BLOCK, ROW_BLOCK)
        row = lax.broadcasted_iota(jnp.int32, (ROW_BLOCK, LANES), 0)
        pos = (i * tm + r0 + row + 1).astype(F32)
        for g, w in enumerate(POOL_WINDOWS):
            div = jnp.minimum(pos, float(w))
            for cc in range(blocks_per_group):
                c = g * blocks_per_group + cc
                cur = slab_ref[c, pl.ds(r0 + POOL_HALO, ROW_BLOCK), :]
                acc = cur
                for j in range(1, w):
                    acc = acc + slab_ref[c, pl.ds(r0 + POOL_HALO - j, ROW_BLOCK), :]
                pooled_ref[pl.ds(r0, ROW_BLOCK), c * LANES:(c + 1) * LANES] = (
                    acc / div - cur).astype(BF16)
        return carry

    lax.fori_loop(0, tm // ROW_BLOCK, row_block, 0)

    for r0 in range(0, tm, OUT_ROWS):
        rows = slice(r0, r0 + OUT_ROWS)
        ys = [jnp.dot(pooled_ref[rows, g * group:(g + 1) * group], pw_ref[g],
                      preferred_element_type=F32) for g in range(len(POOL_WINDOWS))]
        y = (jnp.concatenate(ys, axis=-1) + pb_ref[...]) * ps_ref[...]
        o_ref[0, rows, :] = _layer_norm(DN_ALPHA * x_ref[0, rows, :] + y, g1_ref[...], b1_ref[...])


def _ffn_kernel(x_ref, wg_hbm, wu_hbm, wd_hbm, g2_ref, b2_ref, o_ref,
                wg_ref, wu_ref, wd_ref, stage_ref, sem_ref, *, layer):
    @pl.when(_is_first_step())
    def _():
        _load_weights_as_bf16(layer, [(wg_hbm, wg_ref), (wu_hbm, wu_ref), (wd_hbm, wd_ref)],
                              stage_ref, sem_ref)

    x = x_ref[0]
    xb = x.astype(BF16)
    d_ff = wg_ref.shape[1]
    y = None
    for lo, hi in ((0, FFN_SPLIT), (FFN_SPLIT, d_ff)):
        gate = jnp.dot(xb, wg_ref[:, lo:hi], preferred_element_type=F32)
        up = jnp.dot(xb, wu_ref[:, lo:hi], preferred_element_type=F32)
        hidden = (gate * jax.nn.sigmoid(gate) * up).astype(BF16)
        part = jnp.dot(hidden, wd_ref[lo:hi, :], preferred_element_type=F32)
        y = part if y is None else y + part
    o_ref[0] = _layer_norm(DN_ALPHA * x + y, g2_ref[...], b2_ref[...])


def _layer_slice(stacked, layer):
    zeros = (0,) * (stacked.ndim - 1)
    return pl.BlockSpec((None,) + stacked.shape[1:], lambda b, i: (layer,) + zeros,
                        pipeline_mode=pl.Buffered(1))


def _rows(v):
    return v.reshape(v.shape[0], 1, -1)


_HBM = pl.BlockSpec(memory_space=pl.ANY)


def _row_tile_spec(tm, d):
    return pl.BlockSpec((1, tm, d), lambda b, i: (b, i, 0))


def _weight_staging(n_cols):
    return [pltpu.VMEM((WEIGHT_SLOTS, WEIGHT_CHUNK_ROWS, n_cols), F32),
            pltpu.SemaphoreType.DMA((WEIGHT_SLOTS,))]


def _tiled_call(body, name, x, tm, extra_inputs, extra_specs, params, scratch_shapes):
    bsz, seq, d = x.shape
    return pl.pallas_call(
        body,
        grid=(bsz, seq // tm),
        in_specs=[_row_tile_spec(tm, d)] + extra_specs + [_layer_slice(a, k) for a, k in params],
        out_specs=_row_tile_spec(tm, d),
        out_shape=jax.ShapeDtypeStruct(x.shape, F32),
        scratch_shapes=scratch_shapes,
        compiler_params=pltpu.CompilerParams(dimension_semantics=("arbitrary", "arbitrary"),
                                             vmem_limit_bytes=VMEM_LIMIT_BYTES),
        name=name,
    )(x, *extra_inputs, *[a for a, _ in params])


def _conv_mixer(x, j, i, conv_params, ln_params, tm):
    d = x.shape[-1]
    w_in, b_in, w_dw, b_dw, ln_g, ln_b, w_out, b_out = conv_params
    assert d % WEIGHT_CHUNK_ROWS == 0
    g1, b1, _, _ = ln_params
    params = [(b_in, j), (w_dw, j), (b_dw, j), (ln_g, j), (ln_b, j), (b_out, j), (g1, i), (b1, i)]
    scratch = [pltpu.VMEM((d // LANES, CONV_HALO + tm, LANES), F32), pltpu.VMEM((tm, d), F32),
               pltpu.VMEM((d, 2 * d), BF16), pltpu.VMEM((d, d), BF16)] + _weight_staging(2 * d)
    return _tiled_call(functools.partial(_conv_mixer_kernel, layer=j), "conv_mixer", x, tm,
                       [w_in, w_out], [_HBM] * 2, params, scratch)


def _pool_mixer(x, j, i, pool_params, ln_params, tm):
    d = x.shape[-1]
    p_w, p_b, p_scale = pool_params
    g1, b1, _, _ = ln_params
    halo_blocks_per_tile = tm // POOL_HALO
    prev_spec = pl.BlockSpec(
        (1, POOL_HALO, d), lambda b, t: (b, jnp.maximum(t * halo_blocks_per_tile - 1, 0), 0))
    params = [(p_w, j), (p_b, j), (p_scale, j), (g1, i), (b1, i)]
    scratch = [pltpu.VMEM((d // LANES, POOL_HALO + tm, LANES), F32), pltpu.VMEM((tm, d), BF16)]
    return _tiled_call(_pool_mixer_kernel, "pool_mixer", x, tm, [x], [prev_spec], params, scratch)


def _ffn(x, i, ffn_params, ln_params, tm):
    d = x.shape[-1]
    w_gate, w_up, w_down = ffn_params
    d_ff = w_gate.shape[-1]
    assert 0 < FFN_SPLIT < d_ff and d % WEIGHT_CHUNK_ROWS == 0 and d_ff % WEIGHT_CHUNK_ROWS == 0
    _, _, g2, b2 = ln_params
    scratch = [pltpu.VMEM((d, d_ff), BF16), pltpu.VMEM((d, d_ff), BF16),
               pltpu.VMEM((d_ff, d), BF16)] + _weight_staging(d_ff)
    return _tiled_call(functools.partial(_ffn_kernel, layer=i), "ffn", x, tm,
                       [w_gate, w_up, w_down], [_HBM] * 3, [(g2, i), (b2, i)], scratch)


def kernel(x, a_w_in, a_b_in, a_w_dw, a_b_dw, a_ln_g, a_ln_b, a_w_out, a_b_out, p_w, p_b, p_scale, ffn_w_gate, ffn_w_up, ffn_w_down, ln1_g, ln1_b, ln2_g, ln2_b):
    seq, d = x.shape[1], x.shape[2]
    tm = math.gcd(seq, ROW_TILE)
    assert tm % ROW_BLOCK == 0 and tm >= CONV_HALO
    conv_params = (a_w_in, _rows(a_b_in), a_w_dw.reshape(-1, CONV_WIDTH, d),
                   _rows(a_b_dw), _rows(a_ln_g), _rows(a_ln_b), a_w_out, _rows(a_b_out))
    pool_params = (p_w.astype(BF16), _rows(p_b), _rows(p_scale))
    ffn_params = (ffn_w_gate, ffn_w_up, ffn_w_down)
    ln_params = (_rows(ln1_g), _rows(ln1_b), _rows(ln2_g), _rows(ln2_b))
    for i in range(DEPTH):
        j = i // 2
        if i % 2 == 0:
            x = _conv_mixer(x, j, i, conv_params, ln_params, tm)
        else:
            x = _pool_mixer(x, j, i, pool_params, ln_params, tm)
        x = _ffn(x, i, ffn_params, ln_params, tm)
    return x
```

```python
import functools
import math

import jax
import jax.numpy as jnp
from jax import lax
from jax.experimental import pallas as pl
from jax.experimental.pallas import tpu as pltpu

F32 = jnp.float32
BF16 = jnp.bfloat16

DEPTH = 4
CONV_WIDTH = 31
POOL_WINDOWS = (2, 4, 8, 16)
DN_ALPHA = float((2 * DEPTH) ** 0.25)
LN_EPS = 1e-5

LANES = 128
ROW_TILE = 1024
FFN_SPLIT = 1536
WEIGHT_CHUNK_ROWS = 128
WEIGHT_SLOTS = 6
CONV_ROW_TILE = 512
IN_PROJ_COLS = 256
OUT_ROWS = 512
ROW_BLOCK = 64
CONV_HALO = 32
POOL_HALO = 16
VMEM_LIMIT_BYTES = 60 * 1024 * 1024


def _layer_norm(v, g, b):
    mu = jnp.mean(v, axis=-1, keepdims=True)
    vc = v - mu
    var = jnp.mean(vc * vc, axis=-1, keepdims=True)
    return vc * lax.rsqrt(var + LN_EPS) * g + b


def _bf16_dot(a, b):
    return jnp.dot(a.astype(BF16), b, preferred_element_type=F32)


def _load_weights_as_bf16(layer, weights, stage_ref, sem_ref):
    chunks = [(w_hbm, dst_ref, r0) for w_hbm, dst_ref in weights
              for r0 in range(0, dst_ref.shape[0], WEIGHT_CHUNK_ROWS)]

    def copy(n):
        w_hbm, dst_ref, r0 = chunks[n]
        slot = n % WEIGHT_SLOTS
        src = w_hbm.at[layer, pl.ds(r0, WEIGHT_CHUNK_ROWS), :]
        dst = stage_ref.at[slot, :, pl.ds(0, dst_ref.shape[1])]
        return pltpu.make_async_copy(src, dst, sem_ref.at[slot])

    for n in range(min(WEIGHT_SLOTS - 1, len(chunks))):
        copy(n).start()
    for n, (_, dst_ref, r0) in enumerate(chunks):
        ahead = n + WEIGHT_SLOTS - 1
        if ahead < len(chunks):
            copy(ahead).start()
        copy(n).wait()
        staged = stage_ref[n % WEIGHT_SLOTS, :, 0:dst_ref.shape[1]]
        dst_ref[r0:r0 + WEIGHT_CHUNK_ROWS, :] = staged.astype(BF16)


def _is_first_step():
    return (pl.program_id(0) == 0) & (pl.program_id(1) == 0)


def _conv_mixer_kernel(x_ref, xn_ref, win_hbm, wout_hbm, bin_ref, wdw_ref, bdw_ref, lng_ref, lnb_ref,
                       bout_ref, g1_ref, b1_ref, o_ref, glu_ref, conv_ref, h_ref, xnb_ref,
                       win_ref, wout_ref, stage_ref, sem_ref, *, layer, tiles_per_seq):
    tm, d = conv_ref.shape
    n_lane_blocks = d // LANES
    s = pl.program_id(0)

    @pl.when(s == 0)
    def _():
        _load_weights_as_bf16(layer, [(win_hbm, win_ref), (wout_hbm, wout_ref)],
                              stage_ref, sem_ref)
        h_ref[...] = _bf16_dot(x_ref[...], win_ref[...]) + bin_ref[...]
        glu_ref[:, tm:, :] = jnp.zeros((n_lane_blocks, CONV_HALO, LANES), F32)

    seq_start = (s % tiles_per_seq) == 0
    tail = glu_ref[:, tm:, :]
    glu_ref[:, 0:CONV_HALO, :] = jnp.where(seq_start, 0.0, tail)
    for c in range(n_lane_blocks):
        lanes = slice(c * LANES, (c + 1) * LANES)
        gate_lanes = slice(d + c * LANES, d + (c + 1) * LANES)
        glu_ref[c, CONV_HALO:, :] = h_ref[:, lanes] * jax.nn.sigmoid(h_ref[:, gate_lanes])

    xnb_ref[...] = xn_ref[...].astype(BF16)

    def row_block(r, carry):
        cols = pl.ds(pl.multiple_of(r * IN_PROJ_COLS, IN_PROJ_COLS), IN_PROJ_COLS)
        h_ref[:, cols] = (jnp.dot(xnb_ref[...], win_ref[:, cols], preferred_element_type=F32)
                          + bin_ref[:, cols])
        r0 = pl.multiple_of(r * ROW_BLOCK, ROW_BLOCK)
        for c in range(n_lane_blocks):
            lanes = slice(c * LANES, (c + 1) * LANES)
            acc = None
            for k in range(CONV_WIDTH):
                seg = glu_ref[c, pl.ds(r0 + CONV_HALO - (CONV_WIDTH - 1) + k, ROW_BLOCK), :]
                term = seg * wdw_ref[k:k + 1, lanes]
                acc = term if acc is None else acc + term
            conv_ref[pl.ds(r0, ROW_BLOCK), lanes] = acc
        return carry

    lax.fori_loop(0, tm // ROW_BLOCK, row_block, 0)

    for r0 in range(0, tm, OUT_ROWS):
        rows = slice(r0, r0 + OUT_ROWS)
        u = _layer_norm(conv_ref[rows, :] + bdw_ref[...], lng_ref[...], lnb_ref[...])
        u = u * jax.nn.sigmoid(u)
        mix = _bf16_dot(u, wout_ref[...]) + bout_ref[...]
        o_ref[rows, :] = _layer_norm(DN_ALPHA * x_ref[rows, :] + mix, g1_ref[...], b1_ref[...])


def _pool_mixer_kernel(x_ref, xprev_ref, pw_ref, pb_ref, ps_ref, g1_ref, b1_ref, o_ref,
                       slab_ref, pooled_ref):
    tm, d = pooled_ref.shape
    n_lane_blocks = d // LANES
    group = d // len(POOL_WINDOWS)
    blocks_per_group = group // LANES
    i = pl.program_id(1)
    x = x_ref[0]
    prev = jnp.where(i > 0, xprev_ref[0], 0.0)
    for c in range(n_lane_blocks):
        lanes = slice(c * LANES, (c + 1) * LANES)
        slab_ref[c, 0:POOL_HALO, :] = prev[:, lanes]
        slab_ref[c, POOL_HALO:, :] = x[:, lanes]

    def row_block(r, carry):
        r0 = pl.multiple_of(r * ROW_BLOCK, ROW_BLOCK)
        row = lax.broadcasted_iota(jnp.int32, (ROW_BLOCK, LANES), 0)
        pos = (i * tm + r0 + row + 1).astype(F32)
        for g, w in enumerate(POOL_WINDOWS):
            div = jnp.minimum(pos, float(w))
            for cc in range(blocks_per_group):
                c = g * blocks_per_group + cc
                cur = slab_ref[c, pl.ds(r0 + POOL_HALO, ROW_BLOCK), :]
                acc = cur
                for j in range(1, w):
                    acc = acc + slab_ref[c, pl.ds(r0 + POOL_HALO - j, ROW_BLOCK), :]
                pooled_ref[pl.ds(r0, ROW_BLOCK), c * LANES:(c + 1) * LANES] = (
                    acc / div - cur).astype(BF16)
        return carry

    lax.fori_loop(0, tm // ROW_BLOCK, row_block, 0)

    for r0 in range(0, tm, OUT_ROWS):
        rows = slice(r0, r0 + OUT_ROWS)
        ys = [jnp.dot(pooled_ref[rows, g * group:(g + 1) * group], pw_ref[g],
                      preferred_element_type=F32) for g in range(len(POOL_WINDOWS))]
        y = (jnp.concatenate(ys, axis=-1) + pb_ref[...]) * ps_ref[...]
        o_ref[0, rows, :] = _layer_norm(DN_ALPHA * x_ref[0, rows, :] + y, g1_ref[...], b1_ref[...])


def _ffn_kernel(x_ref, wg_hbm, wu_hbm, wd_hbm, g2_ref, b2_ref, o_ref,
                wg_ref, wu_ref, wd_ref, stage_ref, sem_ref, *, layer):
    @pl.when(_is_first_step())
    def _():
        _load_weights_as_bf16(layer, [(wg_hbm, wg_ref), (wu_hbm, wu_ref), (wd_hbm, wd_ref)],
                              stage_ref, sem_ref)

    x = x_ref[0]
    xb = x.astype(BF16)
    d_ff = wg_ref.shape[1]
    y = None
    for lo, hi in ((0, FFN_SPLIT), (FFN_SPLIT, d_ff)):
        gate = jnp.dot(xb, wg_ref[:, lo:hi], preferred_element_type=F32)
        up = jnp.dot(xb, wu_ref[:, lo:hi], preferred_element_type=F32)
        hidden = (gate * jax.nn.sigmoid(gate) * up).astype(BF16)
        part = jnp.dot(hidden, wd_ref[lo:hi, :], preferred_element_type=F32)
        y = part if y is None else y + part
    o_ref[0] = _layer_norm(DN_ALPHA * x + y, g2_ref[...], b2_ref[...])


def _layer_slice(stacked, layer):
    zeros = (0,) * (stacked.ndim - 1)
    return pl.BlockSpec((None,) + stacked.shape[1:], lambda b, i: (layer,) + zeros,
                        pipeline_mode=pl.Buffered(1))


def _rows(v):
    return v.reshape(v.shape[0], 1, -1)


_HBM = pl.BlockSpec(memory_space=pl.ANY)


def _row_tile_spec(tm, d):
    return pl.BlockSpec((1, tm, d), lambda b, i: (b, i, 0))


def _weight_staging(n_cols):
    return [pltpu.VMEM((WEIGHT_SLOTS, WEIGHT_CHUNK_ROWS, n_cols), F32),
            pltpu.SemaphoreType.DMA((WEIGHT_SLOTS,))]


def _tiled_call(body, name, x, tm, extra_inputs, extra_specs, params, scratch_shapes):
    bsz, seq, d = x.shape
    return pl.pallas_call(
        body,
        grid=(bsz, seq // tm),
        in_specs=[_row_tile_spec(tm, d)] + extra_specs + [_layer_slice(a, k) for a, k in params],
        out_specs=_row_tile_spec(tm, d),
        out_shape=jax.ShapeDtypeStruct(x.shape, F32),
        scratch_shapes=scratch_shapes,
        compiler_params=pltpu.CompilerParams(dimension_semantics=("arbitrary", "arbitrary"),
                                             vmem_limit_bytes=VMEM_LIMIT_BYTES),
        name=name,
    )(x, *extra_inputs, *[a for a, _ in params])


def _conv_mixer(x, j, i, conv_params, ln_params, tm):
    bsz, seq, d = x.shape
    n_tiles = bsz * seq // tm
    w_in, b_in, w_dw, b_dw, ln_g, ln_b, w_out, b_out = conv_params
    assert d % WEIGHT_CHUNK_ROWS == 0 and (tm // ROW_BLOCK) * IN_PROJ_COLS == 2 * d
    g1, b1, _, _ = ln_params
    params = [(b_in, j), (w_dw, j), (b_dw, j), (ln_g, j), (ln_b, j), (b_out, j), (g1, i), (b1, i)]
    x2 = x.reshape(bsz * seq, d)
    tile = pl.BlockSpec((tm, d), lambda s: (s, 0))
    next_tile = pl.BlockSpec((tm, d), lambda s: (jnp.minimum(s + 1, n_tiles - 1), 0))

    def resident(stacked, layer):
        zeros = (0,) * (stacked.ndim - 1)
        return pl.BlockSpec((None,) + stacked.shape[1:], lambda s: (layer,) + zeros,
                            pipeline_mode=pl.Buffered(1))

    out = pl.pallas_call(
        functools.partial(_conv_mixer_kernel, layer=j, tiles_per_seq=seq // tm),
        grid=(n_tiles,),
        in_specs=[tile, next_tile, _HBM, _HBM] + [resident(a, k) for a, k in params],
        out_specs=tile,
        out_shape=jax.ShapeDtypeStruct(x2.shape, F32),
        scratch_shapes=[pltpu.VMEM((d // LANES, CONV_HALO + tm, LANES), F32), pltpu.VMEM((tm, d), F32),
                        pltpu.VMEM((tm, 2 * d), F32), pltpu.VMEM((tm, d), BF16),
                        pltpu.VMEM((d, 2 * d), BF16), pltpu.VMEM((d, d), BF16)]
        + _weight_staging(2 * d),
        compiler_params=pltpu.CompilerParams(dimension_semantics=("arbitrary",),
                                             vmem_limit_bytes=VMEM_LIMIT_BYTES),
        name="conv_mixer",
    )(x2, x2, w_in, w_out, *[a for a, _ in params])
    return out.reshape(x.shape)


def _pool_mixer(x, j, i, pool_params, ln_params, tm):
    d = x.shape[-1]
    p_w, p_b, p_scale = pool_params
    g1, b1, _, _ = ln_params
    halo_blocks_per_tile = tm // POOL_HALO
    prev_spec = pl.BlockSpec(
        (1, POOL_HALO, d), lambda b, t: (b, jnp.maximum(t * halo_blocks_per_tile - 1, 0), 0))
    params = [(p_w, j), (p_b, j), (p_scale, j), (g1, i), (b1, i)]
    scratch = [pltpu.VMEM((d // LANES, POOL_HALO + tm, LANES), F32), pltpu.VMEM((tm, d), BF16)]
    return _tiled_call(_pool_mixer_kernel, "pool_mixer", x, tm, [x], [prev_spec], params, scratch)


def _ffn(x, i, ffn_params, ln_params, tm):
    d = x.shape[-1]
    w_gate, w_up, w_down = ffn_params
    d_ff = w_gate.shape[-1]
    assert 0 < FFN_SPLIT < d_ff and d % WEIGHT_CHUNK_ROWS == 0 and d_ff % WEIGHT_CHUNK_ROWS == 0
    _, _, g2, b2 = ln_params
    scratch = [pltpu.VMEM((d, d_ff), BF16), pltpu.VMEM((d, d_ff), BF16),
               pltpu.VMEM((d_ff, d), BF16)] + _weight_staging(d_ff)
    return _tiled_call(functools.partial(_ffn_kernel, layer=i), "ffn", x, tm,
                       [w_gate, w_up, w_down], [_HBM] * 3, [(g2, i), (b2, i)], scratch)


def kernel(x, a_w_in, a_b_in, a_w_dw, a_b_dw, a_ln_g, a_ln_b, a_w_out, a_b_out, p_w, p_b, p_scale, ffn_w_gate, ffn_w_up, ffn_w_down, ln1_g, ln1_b, ln2_g, ln2_b):
    seq, d = x.shape[1], x.shape[2]
    tm = math.gcd(seq, ROW_TILE)
    assert tm % ROW_BLOCK == 0 and tm >= CONV_HALO
    conv_params = (a_w_in, _rows(a_b_in), a_w_dw.reshape(-1, CONV_WIDTH, d),
                   _rows(a_b_dw), _rows(a_ln_g), _rows(a_ln_b), a_w_out, _rows(a_b_out))
    pool_params = (p_w.astype(BF16), _rows(p_b), _rows(p_scale))
    ffn_params = (ffn_w_gate, ffn_w_up, ffn_w_down)
    ln_params = (_rows(ln1_g), _rows(ln1_b), _rows(ln2_g), _rows(ln2_b))
    for i in range(DEPTH):
        j = i // 2
        if i % 2 == 0:
            x = _conv_mixer(x, j, i, conv_params, ln_params, math.gcd(seq, CONV_ROW_TILE))
        else:
            x = _pool_mixer(x, j, i, pool_params, ln_params, tm)
        x = _ffn(x, i, ffn_params, ln_params, tm)
    return x
```

```python
import functools
import math

import jax
import jax.numpy as jnp
from jax import lax
from jax.experimental import pallas as pl
from jax.experimental.pallas import tpu as pltpu

F32 = jnp.float32
BF16 = jnp.bfloat16

DEPTH = 4
CONV_WIDTH = 31
POOL_WINDOWS = (2, 4, 8, 16)
DN_ALPHA = float((2 * DEPTH) ** 0.25)
LN_EPS = 1e-5

LANES = 128
ROW_TILE = 1024
FFN_SPLIT = 1536
WEIGHT_CHUNK_ROWS = 256
WEIGHT_SLOTS = 4
CONV_GROUP = 256
OUT_ROWS = 512
ROW_BLOCK = 128
CONV_HALO = 32
POOL_HALO = 16
VMEM_LIMIT_BYTES = 60 * 1024 * 1024


def _layer_norm(v, g, b):
    mu = jnp.mean(v, axis=-1, keepdims=True)
    vc = v - mu
    var = jnp.mean(vc * vc, axis=-1, keepdims=True)
    return vc * lax.rsqrt(var + LN_EPS) * g + b


def _bf16_dot(a, b):
    return jnp.dot(a.astype(BF16), b, preferred_element_type=F32)


def _load_weights_as_bf16(layer, weights, stage_ref, sem_ref):
    chunks = [(w_hbm, dst_ref, r0) for w_hbm, dst_ref in weights
              for r0 in range(0, dst_ref.shape[0], WEIGHT_CHUNK_ROWS)]

    def copy(n):
        w_hbm, dst_ref, r0 = chunks[n]
        slot = n % WEIGHT_SLOTS
        src = w_hbm.at[layer, pl.ds(r0, WEIGHT_CHUNK_ROWS), :]
        dst = stage_ref.at[slot, :, pl.ds(0, dst_ref.shape[1])]
        return pltpu.make_async_copy(src, dst, sem_ref.at[slot])

    for n in range(min(WEIGHT_SLOTS - 1, len(chunks))):
        copy(n).start()
    for n, (_, dst_ref, r0) in enumerate(chunks):
        ahead = n + WEIGHT_SLOTS - 1
        if ahead < len(chunks):
            copy(ahead).start()
        copy(n).wait()
        staged = stage_ref[n % WEIGHT_SLOTS, :, 0:dst_ref.shape[1]]
        dst_ref[r0:r0 + WEIGHT_CHUNK_ROWS, :] = staged.astype(BF16)


def _is_first_step():
    return (pl.program_id(0) == 0) & (pl.program_id(1) == 0)


def _conv_mixer_kernel(x_ref, win_hbm, wout_hbm, bin_ref, wdw_ref, bdw_ref, lng_ref, lnb_ref,
                       bout_ref, g1_ref, b1_ref, o_ref, glu_ref, conv_ref, win_ref, wout_ref,
                       stage_ref, sem_ref, *, layer):
    tm, d = conv_ref.shape
    n_lane_blocks = d // LANES
    i = pl.program_id(1)

    @pl.when(_is_first_step())
    def _():
        _load_weights_as_bf16(layer, [(win_hbm, win_ref), (wout_hbm, wout_ref)],
                              stage_ref, sem_ref)

    @pl.when(i == 0)
    def _():
        glu_ref[:, 0:CONV_HALO, :] = jnp.zeros((n_lane_blocks, CONV_HALO, LANES), F32)

    @pl.when(i > 0)
    def _():
        glu_ref[:, 0:CONV_HALO, :] = glu_ref[:, tm:tm + CONV_HALO, :]

    xb = x_ref[0].astype(BF16)
    blocks_per_group = CONV_GROUP // LANES
    for g in range(d // CONV_GROUP):
        cols = slice(g * CONV_GROUP, (g + 1) * CONV_GROUP)
        gate_cols = slice(d + g * CONV_GROUP, d + (g + 1) * CONV_GROUP)
        val = jnp.dot(xb, win_ref[:, cols], preferred_element_type=F32) + bin_ref[:, cols]
        gt = jnp.dot(xb, win_ref[:, gate_cols], preferred_element_type=F32) + bin_ref[:, gate_cols]
        glu = val * jax.nn.sigmoid(gt)
        for cc in range(blocks_per_group):
            glu_ref[g * blocks_per_group + cc, CONV_HALO:, :] = glu[:, cc * LANES:(cc + 1) * LANES]

    def row_block(r, carry):
        r0 = pl.multiple_of(r * ROW_BLOCK, ROW_BLOCK)
        for c in range(n_lane_blocks):
            lanes = slice(c * LANES, (c + 1) * LANES)
            acc = None
            for k in range(CONV_WIDTH):
                seg = glu_ref[c, pl.ds(r0 + CONV_HALO - (CONV_WIDTH - 1) + k, ROW_BLOCK), :]
                term = seg * wdw_ref[k:k + 1, lanes]
                acc = term if acc is None else acc + term
            conv_ref[pl.ds(r0, ROW_BLOCK), lanes] = acc
        return carry

    lax.fori_loop(0, tm // ROW_BLOCK, row_block, 0)

    for r0 in range(0, tm, OUT_ROWS):
        rows = slice(r0, r0 + OUT_ROWS)
        u = _layer_norm(conv_ref[rows, :] + bdw_ref[...], lng_ref[...], lnb_ref[...])
        u = u * jax.nn.sigmoid(u)
        mix = _bf16_dot(u, wout_ref[...]) + bout_ref[...]
        o_ref[0, rows, :] = _layer_norm(DN_ALPHA * x_ref[0, rows, :] + mix, g1_ref[...], b1_ref[...])


def _pool_mixer_kernel(x_ref, xprev_ref, pw_ref, pb_ref, ps_ref, g1_ref, b1_ref, o_ref,
                       slab_ref, pooled_ref):
    tm, d = pooled_ref.shape
    n_lane_blocks = d // LANES
    group = d // len(POOL_WINDOWS)
    blocks_per_group = group // LANES
    i = pl.program_id(1)
    x = x_ref[0]
    prev = jnp.where(i > 0, xprev_ref[0], 0.0)
    for c in range(n_lane_blocks):
        lanes = slice(c * LANES, (c + 1) * LANES)
        slab_ref[c, 0:POOL_HALO, :] = prev[:, lanes]
        slab_ref[c, POOL_HALO:, :] = x[:, lanes]

    def row_block(r, carry):
        r0 = pl.multiple_of(r * ROW_BLOCK, ROW_BLOCK)
        row = lax.broadcasted_iota(jnp.int32, (ROW_BLOCK, LANES), 0)
        pos = (i * tm + r0 + row + 1).astype(F32)
        for g, w in enumerate(POOL_WINDOWS):
            div = jnp.minimum(pos, float(w))
            for cc in range(blocks_per_group):
                c = g * blocks_per_group + cc
                cur = slab_ref[c, pl.ds(r0 + POOL_HALO, ROW_BLOCK), :]
                acc = cur
                for j in range(1, w):
                    acc = acc + slab_ref[c, pl.ds(r0 + POOL_HALO - j, ROW_BLOCK), :]
                pooled_ref[pl.ds(r0, ROW_BLOCK), c * LANES:(c + 1) * LANES] = (
                    acc / div - cur).astype(BF16)
        return carry

    lax.fori_loop(0, tm // ROW_BLOCK, row_block, 0)

    for r0 in range(0, tm, OUT_ROWS):
        rows = slice(r0, r0 + OUT_ROWS)
        ys = [jnp.dot(pooled_ref[rows, g * group:(g + 1) * group], pw_ref[g],
                      preferred_element_type=F32) for g in range(len(POOL_WINDOWS))]
        y = (jnp.concatenate(ys, axis=-1) + pb_ref[...]) * ps_ref[...]
        o_ref[0, rows, :] = _layer_norm(DN_ALPHA * x_ref[0, rows, :] + y, g1_ref[...], b1_ref[...])


def _ffn_kernel(x_ref, wg_hbm, wu_hbm, wd_hbm, g2_ref, b2_ref, o_ref,
                wg_ref, wu_ref, wd_ref, stage_ref, sem_ref, *, layer):
    @pl.when(_is_first_step())
    def _():
        _load_weights_as_bf16(layer, [(wg_hbm, wg_ref), (wu_hbm, wu_ref), (wd_hbm, wd_ref)],
                              stage_ref, sem_ref)

    x = x_ref[0]
    xb = x.astype(BF16)
    d_ff = wg_ref.shape[1]
    y = None
    for lo, hi in ((0, FFN_SPLIT), (FFN_SPLIT, d_ff)):
        gate = jnp.dot(xb, wg_ref[:, lo:hi], preferred_element_type=F32)
        up = jnp.dot(xb, wu_ref[:, lo:hi], preferred_element_type=F32)
        hidden = (gate * jax.nn.sigmoid(gate) * up).astype(BF16)
        part = jnp.dot(hidden, wd_ref[lo:hi, :], preferred_element_type=F32)
        y = part if y is None else y + part
    o_ref[0] = _layer_norm(DN_ALPHA * x + y, g2_ref[...], b2_ref[...])


def _layer_slice(stacked, layer):
    zeros = (0,) * (stacked.ndim - 1)
    return pl.BlockSpec((None,) + stacked.shape[1:], lambda b, i: (layer,) + zeros,
                        pipeline_mode=pl.Buffered(1))


def _rows(v):
    return v.reshape(v.shape[0], 1, -1)


_HBM = pl.BlockSpec(memory_space=pl.ANY)


def _row_tile_spec(tm, d):
    return pl.BlockSpec((1, tm, d), lambda b, i: (b, i, 0))


def _weight_staging(n_cols):
    return [pltpu.VMEM((WEIGHT_SLOTS, WEIGHT_CHUNK_ROWS, n_cols), F32),
            pltpu.SemaphoreType.DMA((WEIGHT_SLOTS,))]


def _tiled_call(body, name, x, tm, extra_inputs, extra_specs, params, scratch_shapes):
    bsz, seq, d = x.shape
    return pl.pallas_call(
        body,
        grid=(bsz, seq // tm),
        in_specs=[_row_tile_spec(tm, d)] + extra_specs + [_layer_slice(a, k) for a, k in params],
        out_specs=_row_tile_spec(tm, d),
        out_shape=jax.ShapeDtypeStruct(x.shape, F32),
        scratch_shapes=scratch_shapes,
        compiler_params=pltpu.CompilerParams(dimension_semantics=("arbitrary", "arbitrary"),
                                             vmem_limit_bytes=VMEM_LIMIT_BYTES),
        name=name,
    )(x, *extra_inputs, *[a for a, _ in params])


def _conv_mixer(x, j, i, conv_params, ln_params, tm):
    d = x.shape[-1]
    w_in, b_in, w_dw, b_dw, ln_g, ln_b, w_out, b_out = conv_params
    assert d % WEIGHT_CHUNK_ROWS == 0
    g1, b1, _, _ = ln_params
    params = [(b_in, j), (w_dw, j), (b_dw, j), (ln_g, j), (ln_b, j), (b_out, j), (g1, i), (b1, i)]
    scratch = [pltpu.VMEM((d // LANES, CONV_HALO + tm, LANES), F32), pltpu.VMEM((tm, d), F32),
               pltpu.VMEM((d, 2 * d), BF16), pltpu.VMEM((d, d), BF16)] + _weight_staging(2 * d)
    return _tiled_call(functools.partial(_conv_mixer_kernel, layer=j), "conv_mixer", x, tm,
                       [w_in, w_out], [_HBM] * 2, params, scratch)


def _pool_mixer(x, j, i, pool_params, ln_params, tm):
    d = x.shape[-1]
    p_w, p_b, p_scale = pool_params
    g1, b1, _, _ = ln_params
    halo_blocks_per_tile = tm // POOL_HALO
    prev_spec = pl.BlockSpec(
        (1, POOL_HALO, d), lambda b, t: (b, jnp.maximum(t * halo_blocks_per_tile - 1, 0), 0))
    params = [(p_w, j), (p_b, j), (p_scale, j), (g1, i), (b1, i)]
    scratch = [pltpu.VMEM((d // LANES, POOL_HALO + tm, LANES), F32), pltpu.VMEM((tm, d), BF16)]
    return _tiled_call(_pool_mixer_kernel, "pool_mixer", x, tm, [x], [prev_spec], params, scratch)


def _ffn(x, i, ffn_params, ln_params, tm):
    d = x.shape[-1]
    w_gate, w_up, w_down = ffn_params
    d_ff = w_gate.shape[-1]
    assert 0 < FFN_SPLIT < d_ff and d % WEIGHT_CHUNK_ROWS == 0 and d_ff % WEIGHT_CHUNK_ROWS == 0
    _, _, g2, b2 = ln_params
    scratch = [pltpu.VMEM((d, d_ff), BF16), pltpu.VMEM((d, d_ff), BF16),
               pltpu.VMEM((d_ff, d), BF16)] + _weight_staging(d_ff)
    return _tiled_call(functools.partial(_ffn_kernel, layer=i), "ffn", x, tm,
                       [w_gate, w_up, w_down], [_HBM] * 3, [(g2, i), (b2, i)], scratch)


def kernel(x, a_w_in, a_b_in, a_w_dw, a_b_dw, a_ln_g, a_ln_b, a_w_out, a_b_out, p_w, p_b, p_scale, ffn_w_gate, ffn_w_up, ffn_w_down, ln1_g, ln1_b, ln2_g, ln2_b):
    seq, d = x.shape[1], x.shape[2]
    tm = math.gcd(seq, ROW_TILE)
    assert tm % ROW_BLOCK == 0 and tm >= CONV_HALO
    conv_params = (a_w_in, _rows(a_b_in), a_w_dw.reshape(-1, CONV_WIDTH, d),
                   _rows(a_b_dw), _rows(a_ln_g), _rows(a_ln_b), a_w_out, _rows(a_b_out))
    pool_params = (p_w.astype(BF16), _rows(p_b), _rows(p_scale))
    ffn_params = (ffn_w_gate, ffn_w_up, ffn_w_down)
    ln_params = (_rows(ln1_g), _rows(ln1_b), _rows(ln2_g), _rows(ln2_b))
    for i in range(DEPTH):
        j = i // 2
        if i % 2 == 0:
            x = _conv_mixer(x, j, i, conv_params, ln_params, tm)
        else:
            x = _pool_mixer(x, j, i, pool_params, ln_params, tm)
        x = _ffn(x, i, ffn_params, ln_params, tm)
    return x
```

```python
import functools
import math

import jax
import jax.numpy as jnp
from jax import lax
from jax.experimental import pallas as pl
from jax.experimental.pallas import tpu as pltpu

F32 = jnp.float32
BF16 = jnp.bfloat16

DEPTH = 4
CONV_WIDTH = 31
POOL_WINDOWS = (2, 4, 8, 16)
DN_ALPHA = float((2 * DEPTH) ** 0.25)
LN_EPS = 1e-5

LANES = 128
ROW_TILE = 1024
FFN_SPLIT = 1536
WEIGHT_CHUNK_ROWS = 256
WEIGHT_SLOTS = 4
CONV_GROUP = 256
OUT_ROWS = 512
ROW_BLOCK = 128
CONV_HALO = 32
POOL_HALO = 16
VMEM_LIMIT_BYTES = 60 * 1024 * 1024


def _layer_norm(v, g, b):
    mu = jnp.mean(v, axis=-1, keepdims=True)
    vc = v - mu
    var = jnp.mean(vc * vc, axis=-1, keepdims=True)
    return vc * lax.rsqrt(var + LN_EPS) * g + b


def _bf16_dot(a, b):
    return jnp.dot(a.astype(BF16), b, preferred_element_type=F32)


def _load_weights_as_bf16(layer, weights, stage_ref, sem_ref):
    chunks = [(w_hbm, dst_ref, r0) for w_hbm, dst_ref in weights
              for r0 in range(0, dst_ref.shape[0], WEIGHT_CHUNK_ROWS)]

    def copy(n):
        w_hbm, dst_ref, r0 = chunks[n]
        slot = n % WEIGHT_SLOTS
        src = w_hbm.at[layer, pl.ds(r0, WEIGHT_CHUNK_ROWS), :]
        dst = stage_ref.at[slot, :, pl.ds(0, dst_ref.shape[1])]
        return pltpu.make_async_copy(src, dst, sem_ref.at[slot])

    for n in range(min(WEIGHT_SLOTS - 1, len(chunks))):
        copy(n).start()
    for n, (_, dst_ref, r0) in enumerate(chunks):
        ahead = n + WEIGHT_SLOTS - 1
        if ahead < len(chunks):
            copy(ahead).start()
        copy(n).wait()
        staged = stage_ref[n % WEIGHT_SLOTS, :, 0:dst_ref.shape[1]]
        dst_ref[r0:r0 + WEIGHT_CHUNK_ROWS, :] = staged.astype(BF16)


def _is_first_step():
    return (pl.program_id(0) == 0) & (pl.program_id(1) == 0)


def _conv_mixer_kernel(x_ref, win_hbm, wout_hbm, bin_ref, wdw_ref, bdw_ref, lng_ref, lnb_ref,
                       bout_ref, g1_ref, b1_ref, o_ref, glu_ref, conv_ref, win_ref, wout_ref,
                       stage_ref, sem_ref, *, layer):
    tm, d = conv_ref.shape
    n_lane_blocks = d // LANES
    i = pl.program_id(1)

    @pl.when(_is_first_step())
    def _():
        _load_weights_as_bf16(layer, [(win_hbm, win_ref), (wout_hbm, wout_ref)],
                              stage_ref, sem_ref)

    @pl.when(i == 0)
    def _():
        glu_ref[:, 0:CONV_HALO, :] = jnp.zeros((n_lane_blocks, CONV_HALO, LANES), F32)

    @pl.when(i > 0)
    def _():
        glu_ref[:, 0:CONV_HALO, :] = glu_ref[:, tm:tm + CONV_HALO, :]

    xb = x_ref[0].astype(BF16)
    blocks_per_group = CONV_GROUP // LANES
    for g in range(d // CONV_GROUP):
        cols = slice(g * CONV_GROUP, (g + 1) * CONV_GROUP)
        gate_cols = slice(d + g * CONV_GROUP, d + (g + 1) * CONV_GROUP)
        val = jnp.dot(xb, win_ref[:, cols], preferred_element_type=F32) + bin_ref[:, cols]
        gt = jnp.dot(xb, win_ref[:, gate_cols], preferred_element_type=F32) + bin_ref[:, gate_cols]
        glu = val * jax.nn.sigmoid(gt)
        for cc in range(blocks_per_group):
            glu_ref[g * blocks_per_group + cc, CONV_HALO:, :] = glu[:, cc * LANES:(cc + 1) * LANES]

    def row_block(r, carry):
        r0 = pl.multiple_of(r * ROW_BLOCK, ROW_BLOCK)
        for c in range(n_lane_blocks):
            lanes = slice(c * LANES, (c + 1) * LANES)
            acc = None
            for k in range(CONV_WIDTH):
                seg = glu_ref[c, pl.ds(r0 + CONV_HALO - (CONV_WIDTH - 1) + k, ROW_BLOCK), :]
                term = seg * wdw_ref[k:k + 1, lanes]
                acc = term if acc is None else acc + term
            conv_ref[pl.ds(r0, ROW_BLOCK), lanes] = acc
        return carry

    lax.fori_loop(0, tm // ROW_BLOCK, row_block, 0)

    for r0 in range(0, tm, OUT_ROWS):
        rows = slice(r0, r0 + OUT_ROWS)
        u = _layer_norm(conv_ref[rows, :] + bdw_ref[...], lng_ref[...], lnb_ref[...])
        u = u * jax.nn.sigmoid(u)
        mix = _bf16_dot(u, wout_ref[...]) + bout_ref[...]
        o_ref[0, rows, :] = _layer_norm(DN_ALPHA * x_ref[0, rows, :] + mix, g1_ref[...], b1_ref[...])


def _pool_mixer_kernel(x_ref, xprev_ref, pw_ref, pb_ref, ps_ref, g1_ref, b1_ref, o_ref,
                       slab_ref, pooled_ref):
    tm, d = pooled_ref.shape
    n_lane_blocks = d // LANES
    group = d // len(POOL_WINDOWS)
    blocks_per_group = group // LANES
    i = pl.program_id(1)
    x = x_ref[0]
    prev = jnp.where(i > 0, xprev_ref[0], 0.0)
    for c in range(n_lane_blocks):
        lanes = slice(c * LANES, (c + 1) * LANES)
        slab_ref[c, 0:POOL_HALO, :] = prev[:, lanes]
        slab_ref[c, POOL_HALO:, :] = x[:, lanes]

    def row_block(r, carry):
        r0 = pl.multiple_of(r * ROW_BLOCK, ROW_BLOCK)
        row = lax.broadcasted_iota(jnp.int32, (ROW_BLOCK, LANES), 0)
        pos = (i * tm + r0 + row + 1).astype(F32)
        for g, w in enumerate(POOL_WINDOWS):
            div = jnp.minimum(pos, float(w))
            for cc in range(blocks_per_group):
                c = g * blocks_per_group + cc
                cur = slab_ref[c, pl.ds(r0 + POOL_HALO, ROW_BLOCK), :]
                acc = cur
                for j in range(1, w):
                    acc = acc + slab_ref[c, pl.ds(r0 + POOL_HALO - j, ROW_BLOCK), :]
                pooled_ref[pl.ds(r0, ROW_BLOCK), c * LANES:(c + 1) * LANES] = (
                    acc / div - cur).astype(BF16)
        return carry

    lax.fori_loop(0, tm // ROW_BLOCK, row_block, 0)

    for r0 in range(0, tm, OUT_ROWS):
        rows = slice(r0, r0 + OUT_ROWS)
        ys = [jnp.dot(pooled_ref[rows, g * group:(g + 1) * group], pw_ref[g],
                      preferred_element_type=F32) for g in range(len(POOL_WINDOWS))]
        y = (jnp.concatenate(ys, axis=-1) + pb_ref[...]) * ps_ref[...]
        o_ref[0, rows, :] = _layer_norm(DN_ALPHA * x_ref[0, rows, :] + y, g1_ref[...], b1_ref[...])


def _ffn_kernel(x_ref, wg_hbm, wu_hbm, wd_hbm, g2_ref, b2_ref, o_ref,
                wg_ref, wu_ref, wd_ref, stage_ref, sem_ref, *, layer):
    @pl.when(_is_first_step())
    def _():
        _load_weights_as_bf16(layer, [(wg_hbm, wg_ref), (wu_hbm, wu_ref), (wd_hbm, wd_ref)],
                              stage_ref, sem_ref)

    x = x_ref[0]
    xb = x.astype(BF16)
    d_ff = wg_ref.shape[1]
    hidden = []
    for lo, hi in ((0, FFN_SPLIT), (FFN_SPLIT, d_ff)):
        gate = jnp.dot(xb, wg_ref[:, lo:hi], preferred_element_type=F32)
        up = jnp.dot(xb, wu_ref[:, lo:hi], preferred_element_type=F32)
        hidden.append((lo, hi, (gate * jax.nn.sigmoid(gate) * up).astype(BF16)))
    for r0 in range(0, x.shape[0], OUT_ROWS):
        rows = slice(r0, r0 + OUT_ROWS)
        y = None
        for lo, hi, h in hidden:
            part = jnp.dot(h[rows], wd_ref[lo:hi, :], preferred_element_type=F32)
            y = part if y is None else y + part
        o_ref[0, rows, :] = _layer_norm(DN_ALPHA * x_ref[0, rows, :] + y, g2_ref[...], b2_ref[...])


def _layer_slice(stacked, layer):
    zeros = (0,) * (stacked.ndim - 1)
    return pl.BlockSpec((None,) + stacked.shape[1:], lambda b, i: (layer,) + zeros,
                        pipeline_mode=pl.Buffered(1))


def _rows(v):
    return v.reshape(v.shape[0], 1, -1)


_HBM = pl.BlockSpec(memory_space=pl.ANY)


def _row_tile_spec(tm, d):
    return pl.BlockSpec((1, tm, d), lambda b, i: (b, i, 0))


def _weight_staging(n_cols):
    return [pltpu.VMEM((WEIGHT_SLOTS, WEIGHT_CHUNK_ROWS, n_cols), F32),
            pltpu.SemaphoreType.DMA((WEIGHT_SLOTS,))]


def _tiled_call(body, name, x, tm, extra_inputs, extra_specs, params, scratch_shapes):
    bsz, seq, d = x.shape
    return pl.pallas_call(
        body,
        grid=(bsz, seq // tm),
        in_specs=[_row_tile_spec(tm, d)] + extra_specs + [_layer_slice(a, k) for a, k in params],
        out_specs=_row_tile_spec(tm, d),
        out_shape=jax.ShapeDtypeStruct(x.shape, F32),
        scratch_shapes=scratch_shapes,
        compiler_params=pltpu.CompilerParams(dimension_semantics=("arbitrary", "arbitrary"),
                                             vmem_limit_bytes=VMEM_LIMIT_BYTES),
        name=name,
    )(x, *extra_inputs, *[a for a, _ in params])


def _conv_mixer(x, j, i, conv_params, ln_params, tm):
    d = x.shape[-1]
    w_in, b_in, w_dw, b_dw, ln_g, ln_b, w_out, b_out = conv_params
    assert d % WEIGHT_CHUNK_ROWS == 0
    g1, b1, _, _ = ln_params
    params = [(b_in, j), (w_dw, j), (b_dw, j), (ln_g, j), (ln_b, j), (b_out, j), (g1, i), (b1, i)]
    scratch = [pltpu.VMEM((d // LANES, CONV_HALO + tm, LANES), F32), pltpu.VMEM((tm, d), F32),
               pltpu.VMEM((d, 2 * d), BF16), pltpu.VMEM((d, d), BF16)] + _weight_staging(2 * d)
    return _tiled_call(functools.partial(_conv_mixer_kernel, layer=j), "conv_mixer", x, tm,
                       [w_in, w_out], [_HBM] * 2, params, scratch)


def _pool_mixer(x, j, i, pool_params, ln_params, tm):
    d = x.shape[-1]
    p_w, p_b, p_scale = pool_params
    g1, b1, _, _ = ln_params
    halo_blocks_per_tile = tm // POOL_HALO
    prev_spec = pl.BlockSpec(
        (1, POOL_HALO, d), lambda b, t: (b, jnp.maximum(t * halo_blocks_per_tile - 1, 0), 0))
    params = [(p_w, j), (p_b, j), (p_scale, j), (g1, i), (b1, i)]
    scratch = [pltpu.VMEM((d // LANES, POOL_HALO + tm, LANES), F32), pltpu.VMEM((tm, d), BF16)]
    return _tiled_call(_pool_mixer_kernel, "pool_mixer", x, tm, [x], [prev_spec], params, scratch)


def _ffn(x, i, ffn_params, ln_params, tm):
    d = x.shape[-1]
    w_gate, w_up, w_down = ffn_params
    d_ff = w_gate.shape[-1]
    assert 0 < FFN_SPLIT < d_ff and d % WEIGHT_CHUNK_ROWS == 0 and d_ff % WEIGHT_CHUNK_ROWS == 0
    _, _, g2, b2 = ln_params
    scratch = [pltpu.VMEM((d, d_ff), BF16), pltpu.VMEM((d, d_ff), BF16),
               pltpu.VMEM((d_ff, d), BF16)] + _weight_staging(d_ff)
    return _tiled_call(functools.partial(_ffn_kernel, layer=i), "ffn", x, tm,
                       [w_gate, w_up, w_down], [_HBM] * 3, [(g2, i), (b2, i)], scratch)


def kernel(x, a_w_in, a_b_in, a_w_dw, a_b_dw, a_ln_g, a_ln_b, a_w_out, a_b_out, p_w, p_b, p_scale, ffn_w_gate, ffn_w_up, ffn_w_down, ln1_g, ln1_b, ln2_g, ln2_b):
    seq, d = x.shape[1], x.shape[2]
    tm = math.gcd(seq, ROW_TILE)
    assert tm % ROW_BLOCK == 0 and tm >= CONV_HALO
    conv_params = (a_w_in, _rows(a_b_in), a_w_dw.reshape(-1, CONV_WIDTH, d),
                   _rows(a_b_dw), _rows(a_ln_g), _rows(a_ln_b), a_w_out, _rows(a_b_out))
    pool_params = (p_w.astype(BF16), _rows(p_b), _rows(p_scale))
    ffn_params = (ffn_w_gate, ffn_w_up, ffn_w_down)
    ln_params = (_rows(ln1_g), _rows(ln1_b), _rows(ln2_g), _rows(ln2_b))
    for i in range(DEPTH):
        j = i // 2
        if i % 2 == 0:
            x = _conv_mixer(x, j, i, conv_params, ln_params, tm)
        else:
            x = _pool_mixer(x, j, i, pool_params, ln_params, tm)
        x = _ffn(x, i, ffn_params, ln_params, tm)
    return x
```
